```python
import jax, jax.numpy as jnp
from jax import lax
import numpy as np

D_MODEL = 1024
BATCH = 16
SEQ = 2048
DEPTH = 2

HEAD_DIM = 64
N_FOX_HEADS = 6
N_RET_HEADS = 6
N_SB_HEADS = 4
N_HEADS = N_FOX_HEADS + N_RET_HEADS + N_SB_HEADS
D_MIX = N_HEADS * HEAD_DIM
D_FOX = N_FOX_HEADS * HEAD_DIM
D_RET = N_RET_HEADS * HEAD_DIM
D_SB = N_SB_HEADS * HEAD_DIM
D_IN = 4 * D_MIX + N_FOX_HEADS
Q_BLOCK = 128
RET_CHUNK = 128
ROPE_BASE = 10000.0
LN_EPS = 1e-5
GN_EPS = 1e-5
DEEPNORM_ALPHA = (2 * DEPTH) ** 0.25
DEEPNORM_BETA = (8 * DEPTH) ** -0.25
FGATE_BIAS_MEAN = 3.0

kernel_name = 'hybrid_fox_retnet_stickbreaking'


def _layernorm(x, gain, bias):
    xf = x.astype(jnp.float32)
    mu = jnp.mean(xf, axis=-1, keepdims=True)
    var = jnp.mean(jnp.square(xf - mu), axis=-1, keepdims=True)
    return ((xf - mu) * lax.rsqrt(var + LN_EPS) * gain + bias).astype(x.dtype)


def _rotary(t, pos):
    half = t.shape[-1] // 2
    inv_freq = 1.0 / (ROPE_BASE ** (jnp.arange(half, dtype=jnp.float32) / half))
    ang = pos[:, None] * inv_freq[None, :]
    cos, sin = jnp.cos(ang), jnp.sin(ang)
    t1, t2 = t[..., :half], t[..., half:]
    return jnp.concatenate([t1 * cos - t2 * sin, t1 * sin + t2 * cos], axis=-1)


def _to_blocks(t, n_blocks, block):
    B, H = t.shape[0], t.shape[1]
    t = t.reshape((B, H, n_blocks, block) + t.shape[3:])
    return jnp.moveaxis(t, 2, 0)


def _from_blocks(o):
    nb, B, H, blk, d = o.shape
    return o.transpose(1, 0, 3, 2, 4).reshape(B, nb * blk, H * d)


def _fox_attention(q, k, v, c):
    B, H, S, d = q.shape
    nb = S // Q_BLOCK
    scale = d ** -0.5
    k_pos = jnp.arange(S)

    def block(args):
        qi, ci, i = args
        q_pos = i * Q_BLOCK + jnp.arange(Q_BLOCK)
        s = jnp.einsum('bhqd,bhkd->bhqk', qi, k).astype(jnp.float32) * scale
        s = s + ci[..., None] - c[:, :, None, :]
        s = jnp.where(k_pos[None, :] <= q_pos[:, None], s, -jnp.inf)
        p = jax.nn.softmax(s, axis=-1)
        return jnp.einsum('bhqk,bhkd->bhqd', p.astype(v.dtype), v)

    o = lax.map(block, (_to_blocks(q, nb, Q_BLOCK), _to_blocks(c, nb, Q_BLOCK), jnp.arange(nb)))
    return _from_blocks(o)


def _stick_breaking(q, k, v):
    B, H, S, d = q.shape
    nb = S // Q_BLOCK
    scale = d ** -0.5
    k_pos = jnp.arange(S)

    def block(args):
        qi, i = args
        q_pos = i * Q_BLOCK + jnp.arange(Q_BLOCK)
        z = jnp.einsum('bhqd,bhkd->bhqk', qi, k).astype(jnp.float32) * scale
        strict = k_pos[None, :] < q_pos[:, None]
        log_rem = jnp.where(strict, jax.nn.log_sigmoid(-z), 0.0)
        tail = lax.cumsum(log_rem, axis=3, reverse=True) - log_rem
        log_a = jax.nn.log_sigmoid(z) + tail
        a = jnp.where(strict, jnp.exp(log_a), 0.0)
        return jnp.einsum('bhqk,bhkd->bhqd', a.astype(v.dtype), v)

    o = lax.map(block, (_to_blocks(q, nb, Q_BLOCK), jnp.arange(nb)))
    return _from_blocks(o)


def _retention(q, k, v, gn_gain):
    B, H, S, d = q.shape
    C = RET_CHUNK
    nc = S // C
    pos = jnp.arange(S, dtype=jnp.float32)
    q = _rotary(q, pos)
    k = _rotary(k, pos) * (d ** -0.5)
    log_g = jnp.log(1.0 - 2.0 ** (-5.0 - jnp.arange(H, dtype=jnp.float32)))
    idx = jnp.arange(C, dtype=jnp.float32)
    diff = idx[:, None] - idx[None, :]
    intra_decay = jnp.where(diff >= 0, jnp.exp(jnp.maximum(diff, 0.0)[None] * log_g[:, None, None]), 0.0)
    query_decay = jnp.exp((idx[None, :] + 1.0) * log_g[:, None])
    key_decay = jnp.exp((C - 1.0 - idx[None, :]) * log_g[:, None])
    chunk_decay = jnp.exp(C * log_g)

    def step(state, inp):
        qc, kc, vc = inp
        scores = jnp.einsum('bhnd,bhmd->bhnm', qc, kc) * intra_decay
        o = jnp.einsum('bhnm,bhmv->bhnv', scores, vc)
        o = o + jnp.einsum('bhnd,bhdv->bhnv', qc, state) * query_decay[:, :, None]
        new_state = chunk_decay[:, None, None] * state + jnp.einsum(
            'bhmd,bhmv->bhdv', kc * key_decay[:, :, None], vc)
        return new_state, o

    state0 = jnp.zeros((B, H, d, d), jnp.float32)
    _, o = lax.scan(step, state0, (_to_blocks(q, nc, C), _to_blocks(k, nc, C), _to_blocks(v, nc, C)))
    o = o.transpose(1, 0, 3, 2, 4)
    o = o.reshape(B, S, H, d)
    mu = jnp.mean(o, axis=-1, keepdims=True)
    var = jnp.mean(jnp.square(o - mu), axis=-1, keepdims=True)
    o = (o - mu) * lax.rsqrt(var + GN_EPS) * gn_gain.astype(jnp.float32).reshape(H, d)
    return o.reshape(B, S, H * d)


def _hybrid_layer(x, w_in, b_fgate, gn_gain, w_out, ln_gain, ln_bias):
    B, S, _ = x.shape
    proj = jnp.einsum('bsd,de->bse', x, w_in)
    q = proj[..., :D_MIX]
    k = proj[..., D_MIX:2 * D_MIX]
    v = proj[..., 2 * D_MIX:3 * D_MIX]
    gate = proj[..., 3 * D_MIX:4 * D_MIX]
    f_logit = (proj[..., 4 * D_MIX:] + b_fgate).astype(jnp.float32)

    def heads(t):
        return t.reshape(B, S, N_HEADS, HEAD_DIM).transpose(0, 2, 1, 3)

    q, k, v = heads(q), heads(k), heads(v)
    f0, f1 = 0, N_FOX_HEADS
    r0, r1 = N_FOX_HEADS, N_FOX_HEADS + N_RET_HEADS

    c = lax.cumsum(jax.nn.log_sigmoid(f_logit), axis=1).transpose(0, 2, 1)
    o_fox = _fox_attention(q[:, f0:f1], k[:, f0:f1], v[:, f0:f1], c)
    o_ret = _retention(q[:, r0:r1].astype(jnp.float32), k[:, r0:r1].astype(jnp.float32),
                       v[:, r0:r1].astype(jnp.float32), gn_gain).astype(x.dtype)
    o_sb = _stick_breaking(q[:, r1:], k[:, r1:], v[:, r1:])

    y = jnp.concatenate([o_fox.astype(x.dtype), o_ret, o_sb.astype(x.dtype)], axis=-1) * jax.nn.silu(gate)
    out = jnp.einsum('bse,ed->bsd', y, w_out)
    return _layernorm(DEEPNORM_ALPHA * x + out, ln_gain, ln_bias)


def setup_inputs(seed: int = 0) -> dict:
    key = jax.random.key(seed)
    ks = jax.random.split(key, 10)
    std_in = D_MODEL ** -0.5
    x = jax.random.normal(ks[0], (BATCH, SEQ, D_MODEL), jnp.float32)
    w_qk = jax.random.normal(ks[1], (DEPTH, D_MODEL, 2 * D_MIX), jnp.float32) * std_in
    w_v = jax.random.normal(ks[2], (DEPTH, D_MODEL, D_MIX), jnp.float32) * (std_in * DEEPNORM_BETA)
    w_g = jax.random.normal(ks[3], (DEPTH, D_MODEL, D_MIX), jnp.float32) * std_in
    w_f = jax.random.normal(ks[4], (DEPTH, D_MODEL, N_FOX_HEADS), jnp.float32) * std_in
    w_in = jnp.concatenate([w_qk, w_v, w_g, w_f], axis=-1)
    b_fgate = FGATE_BIAS_MEAN + jax.random.normal(ks[5], (DEPTH, N_FOX_HEADS), jnp.float32)
    ret_gn_gain = 1.0 + 0.02 * jax.random.normal(ks[6], (DEPTH, D_RET), jnp.float32)
    w_out = jax.random.normal(ks[7], (DEPTH, D_MIX, D_MODEL), jnp.float32) * (D_MIX ** -0.5 * DEEPNORM_BETA)
    ln_gain = 1.0 + 0.02 * jax.random.normal(ks[8], (DEPTH, D_MODEL), jnp.float32)
    ln_bias = 0.02 * jax.random.normal(ks[9], (DEPTH, D_MODEL), jnp.float32)
    return {'x': x, 'w_in': w_in, 'b_fgate': b_fgate, 'ret_gn_gain': ret_gn_gain,
            'w_out': w_out, 'ln_gain': ln_gain, 'ln_bias': ln_bias}


def reference(x, w_in, b_fgate, ret_gn_gain, w_out, ln_gain, ln_bias):
    for layer in range(DEPTH):
        x = _hybrid_layer(x, w_in[layer], b_fgate[layer], ret_gn_gain[layer],
                          w_out[layer], ln_gain[layer], ln_bias[layer])
    return x
```

```python
import functools

import jax
import jax.numpy as jnp
import numpy as np
from jax import lax
from jax.experimental import pallas as pl
from jax.experimental.pallas import tpu as pltpu

F32 = jnp.float32
BF16 = jnp.bfloat16

HEAD_DIM = 64
N_FOX_HEADS = 6
N_RET_HEADS = 6
N_SB_HEADS = 4
N_HEADS = N_FOX_HEADS + N_RET_HEADS + N_SB_HEADS
D_MIX = N_HEADS * HEAD_DIM
ROPE_BASE = 10000.0
LN_EPS = 1e-5
GN_EPS = 1e-5
QK_SCALE = HEAD_DIM ** -0.5

LANES = 128
PAIR = LANES // HEAD_DIM
N_FOX_PAIRS = N_FOX_HEADS // PAIR
N_RET_PAIRS = N_RET_HEADS // PAIR
N_SB_PAIRS = N_SB_HEADS // PAIR
N_PAIRS = N_HEADS // PAIR
F_ROWS = 8

TILE = 256
PROJ_TM = 512
PROJ_TN = 512
VMEM_LIMIT = 48 * 1024 * 1024


def _dot(a, b):
    return jnp.dot(a, b, preferred_element_type=F32)


def _dot_nt(a, b):
    return lax.dot_general(a, b, (((1,), (1,)), ((), ())), preferred_element_type=F32)


def _dot_tn(a, b):
    return lax.dot_general(a, b, (((0,), (0,)), ((), ())), preferred_element_type=F32)


def _split_bf16(x, parts):
    out = []
    for _ in range(parts - 1):
        p = x.astype(BF16)
        out.append(p)
        x = x - p.astype(F32)
    out.append(x.astype(BF16))
    return out


def _log_sigmoid_pair(z):
    lsp = jnp.minimum(z, 0.0) - jnp.log1p(jnp.exp(-jnp.abs(z)))
    return lsp, lsp - z


def _stack_heads(q):
    lane = lax.broadcasted_iota(jnp.int32, q.shape, 1)
    zero = jnp.zeros_like(q)
    return jnp.concatenate([jnp.where(lane < HEAD_DIM, q, zero),
                            jnp.where(lane >= HEAD_DIM, q, zero)], axis=0)


def _unstack_heads(a):
    t = a.shape[0] // 2
    lane = lax.broadcasted_iota(jnp.int32, (t, LANES), 1)
    return jnp.where(lane < HEAD_DIM, a[:t], a[t:])


def _proj_kernel(x_ref, w_ref, wft_ref, qkvg_ref, ft_ref):
    xb = x_ref[...].astype(BF16)
    n_total = w_ref.shape[1]
    for n in range(0, n_total, PROJ_TN):
        qkvg_ref[:, n:n + PROJ_TN] = _dot(xb, w_ref[:, n:n + PROJ_TN]).astype(BF16)
    ft_ref[...] = _dot_nt(wft_ref[...], xb)


def _proj(x2d, w_main, wf_t):
    m, d = x2d.shape
    n = w_main.shape[1]
    return pl.pallas_call(
        _proj_kernel,
        grid=(m // PROJ_TM,),
        in_specs=[
            pl.BlockSpec((PROJ_TM, d), lambda i: (i, 0)),
            pl.BlockSpec((d, n), lambda i: (0, 0)),
            pl.BlockSpec((F_ROWS, d), lambda i: (0, 0)),
        ],
        out_specs=[
            pl.BlockSpec((PROJ_TM, n), lambda i: (i, 0)),
            pl.BlockSpec((F_ROWS, PROJ_TM), lambda i: (0, i)),
        ],
        out_shape=[
            jax.ShapeDtypeStruct((m, n), BF16),
            jax.ShapeDtypeStruct((F_ROWS, m), F32),
        ],
        compiler_params=pltpu.CompilerParams(
            dimension_semantics=("parallel",), vmem_limit_bytes=VMEM_LIMIT),
        name="proj",
    )(x2d, w_main, wf_t)


def _fox_kernel(q_ref, k_ref, v_ref, ft_ref, bf_ref, o_ref, c_scr):
    s_len = q_ref.shape[1]
    nt = s_len // TILE
    pair = pl.program_id(1)

    f_all = ft_ref[...] + bf_ref[...]
    f_row = lax.broadcasted_iota(jnp.int32, f_all.shape, 0)
    f_sel = jnp.zeros_like(f_all)
    for h in range(PAIR):
        picked = jnp.sum(jnp.where(f_row == PAIR * pair + h, f_all, 0.0), axis=0, keepdims=True)
        f_sel = jnp.where(f_row == h, picked, f_sel)
    lsg, _ = _log_sigmoid_pair(f_sel)
    r = lax.broadcasted_iota(jnp.int32, (TILE, TILE), 0)
    c = lax.broadcasted_iota(jnp.int32, (TILE, TILE), 1)
    tri = jnp.where(r <= c, 1.0, 0.0).astype(BF16)
    tri3 = jnp.concatenate([tri, tri, tri], axis=0)
    carry = jnp.zeros((F_ROWS, 1), F32)
    for jb in range(nt):
        blk = lsg[:, jb * TILE:(jb + 1) * TILE]
        pieces = jnp.concatenate(_split_bf16(blk, 3), axis=1)
        cblk = _dot(pieces, tri3) + carry
        c_scr[jb] = cblk
        carry = cblk[:, TILE - 1:TILE]

    row = lax.broadcasted_iota(jnp.int32, (TILE, TILE), 0)
    col = lax.broadcasted_iota(jnp.int32, (TILE, TILE), 1)
    causal = col <= row

    def q_body(qi, _):
        q0 = pl.multiple_of(qi * TILE, TILE)
        qq = _stack_heads(q_ref[0, pl.ds(q0, TILE), :] * QK_SCALE)

        def kv_step(j, carry, masked):
            m_prev, l_prev, acc = carry
            k0 = pl.multiple_of(j * TILE, TILE)
            kb = k_ref[0, pl.ds(k0, TILE), :]
            vb = v_ref[0, pl.ds(k0, TILE), :]
            s = _dot_nt(qq, kb)
            cj = c_scr[j]
            ps, ms, ls = [], [], []
            for h in range(PAIR):
                sh = s[h * TILE:(h + 1) * TILE] - cj[h:h + 1]
                if masked:
                    sh = jnp.where(causal, sh, -jnp.inf)
                mp = m_prev[h * TILE:(h + 1) * TILE]
                mn = jnp.maximum(mp, jnp.max(sh, axis=-1, keepdims=True))
                p = jnp.exp(sh - mn)
                ps.append(p.astype(BF16))
                ms.append(mn)
                ls.append(jnp.sum(p, axis=-1, keepdims=True))
            m_new = jnp.concatenate(ms, axis=0)
            alpha = jnp.exp(m_prev - m_new)
            l_new = alpha * l_prev + jnp.concatenate(ls, axis=0)
            acc = alpha * acc + _dot(jnp.concatenate(ps, axis=0), vb)
            return m_new, l_new, acc

        init = (jnp.full((PAIR * TILE, 1), -jnp.inf, F32),
                jnp.zeros((PAIR * TILE, 1), F32),
                jnp.zeros((PAIR * TILE, LANES), F32))
        carry = lax.fori_loop(0, qi, functools.partial(kv_step, masked=False), init)
        _, l_fin, acc = kv_step(qi, carry, masked=True)
        o_ref[0, pl.ds(q0, TILE), :] = _unstack_heads(acc / l_fin).astype(o_ref.dtype)
        return 0

    lax.fori_loop(0, nt, q_body, 0)


def _fox(qkvg, f_t, b_f, batch, s_len):
    nt = s_len // TILE
    blk = (1, s_len, LANES)
    return pl.pallas_call(
        _fox_kernel,
        grid=(batch, N_FOX_PAIRS),
        in_specs=[
            pl.BlockSpec(blk, lambda b, p: (b, 0, p)),
            pl.BlockSpec(blk, lambda b, p: (b, 0, N_PAIRS + p)),
            pl.BlockSpec(blk, lambda b, p: (b, 0, 2 * N_PAIRS + p)),
            pl.BlockSpec((F_ROWS, s_len), lambda b, p: (0, b)),
            pl.BlockSpec((F_ROWS, 1), lambda b, p: (0, 0)),
        ],
        out_specs=pl.BlockSpec(blk, lambda b, p: (b, 0, p)),
        out_shape=jax.ShapeDtypeStruct((batch, s_len, N_FOX_PAIRS * LANES), BF16),
        scratch_shapes=[pltpu.VMEM((nt, F_ROWS, TILE), F32)],
        compiler_params=pltpu.CompilerParams(
            dimension_semantics=("parallel", "parallel"), vmem_limit_bytes=VMEM_LIMIT),
        name="fox",
    )(qkvg, qkvg, qkvg, f_t, b_f)


def _sb_kernel(q_ref, k_ref, v_ref, o_ref):
    s_len = q_ref.shape[1]
    nt = s_len // TILE

    r = lax.broadcasted_iota(jnp.int32, (TILE, TILE), 0)
    c = lax.broadcasted_iota(jnp.int32, (TILE, TILE), 1)
    after = jnp.where(r > c, 1.0, 0.0).astype(BF16)
    after2 = jnp.concatenate([after, after], axis=0)
    strict1 = c < r
    strict = jnp.concatenate([strict1, strict1], axis=0)

    def q_body(qi, _):
        q0 = pl.multiple_of(qi * TILE, TILE)
        qq = _stack_heads(q_ref[0, pl.ds(q0, TILE), :] * QK_SCALE)

        def kv_step(j, carry, masked):
            tail, acc = carry
            k0 = pl.multiple_of(j * TILE, TILE)
            kb = k_ref[0, pl.ds(k0, TILE), :]
            vb = v_ref[0, pl.ds(k0, TILE), :]
            z = _dot_nt(qq, kb)
            lsp, lsn = _log_sigmoid_pair(z)
            if masked:
                lsn = jnp.where(strict, lsn, 0.0)
            hi, lo = _split_bf16(lsn, 2)
            cum = _dot(jnp.concatenate([hi, lo], axis=1), after2)
            a = jnp.exp(lsp + cum + tail)
            if masked:
                a = jnp.where(strict, a, 0.0)
            acc = acc + _dot(a.astype(BF16), vb)
            tail = tail + cum[:, 0:1] + lsn[:, 0:1]
            return tail, acc

        init = (jnp.zeros((PAIR * TILE, 1), F32), jnp.zeros((PAIR * TILE, LANES), F32))
        carry = kv_step(qi, init, masked=True)

        def back_step(jj, carry):
            return kv_step(qi - 1 - jj, carry, masked=False)

        _, acc = lax.fori_loop(0, qi, back_step, carry)
        o_ref[0, pl.ds(q0, TILE), :] = _unstack_heads(acc).astype(o_ref.dtype)
        return 0

    lax.fori_loop(0, nt, q_body, 0)


def _sb(qkvg, batch, s_len):
    blk = (1, s_len, LANES)
    first = N_FOX_PAIRS + N_RET_PAIRS
    return pl.pallas_call(
        _sb_kernel,
        grid=(batch, N_SB_PAIRS),
        in_specs=[
            pl.BlockSpec(blk, lambda b, p: (b, 0, first + p)),
            pl.BlockSpec(blk, lambda b, p: (b, 0, N_PAIRS + first + p)),
            pl.BlockSpec(blk, lambda b, p: (b, 0, 2 * N_PAIRS + first + p)),
        ],
        out_specs=pl.BlockSpec(blk, lambda b, p: (b, 0, p)),
        out_shape=jax.ShapeDtypeStruct((batch, s_len, N_SB_PAIRS * LANES), BF16),
        compiler_params=pltpu.CompilerParams(
            dimension_semantics=("parallel", "parallel"), vmem_limit_bytes=VMEM_LIMIT),
        name="stick_breaking",
    )(qkvg, qkvg, qkvg)


def _rotary(t, cos, sin_signed):
    lane = lax.broadcasted_iota(jnp.int32, t.shape, 1)
    half = HEAD_DIM // 2
    partner = jnp.where((lane & (HEAD_DIM - 1)) < half,
                        pltpu.roll(t, LANES - half, 1),
                        pltpu.roll(t, half, 1))
    return t * cos + partner * sin_signed


def _ret_kernel(lg_ref, q_ref, k_ref, v_ref, cos_ref, sin_ref, gain_ref, o_ref,
                qr_scr, kr_scr, dec_scr, state_scr):
    s_len = q_ref.shape[1]
    nt = s_len // TILE
    pair = pl.program_id(1)
    lg0 = lg_ref[PAIR * pair]
    lg1 = lg_ref[PAIR * pair + 1]

    lane_row = lax.broadcasted_iota(jnp.int32, (1, LANES), 1)
    lg_lane = jnp.where(lane_row < HEAD_DIM, lg0, lg1)
    idx = lax.broadcasted_iota(jnp.int32, (TILE, LANES), 0).astype(F32)
    query_decay = jnp.exp((idx + 1.0) * lg_lane)
    key_decay = jnp.exp((TILE - 1.0 - idx) * lg_lane)
    chunk_decay = jnp.exp(float(TILE) * lg_lane)

    r = lax.broadcasted_iota(jnp.int32, (TILE, TILE), 0)
    c = lax.broadcasted_iota(jnp.int32, (TILE, TILE), 1)
    diff = jnp.maximum(r - c, 0).astype(F32)
    for h, lg in enumerate((lg0, lg1)):
        dec_scr[h * TILE:(h + 1) * TILE, :] = jnp.where(r >= c, jnp.exp(diff * lg), 0.0)

    rr = lax.broadcasted_iota(jnp.int32, (LANES, LANES), 0)
    cc = lax.broadcasted_iota(jnp.int32, (LANES, LANES), 1)
    same_head = (rr < HEAD_DIM) == (cc < HEAD_DIM)

    def rot_body(i, _):
        r0 = pl.multiple_of(i * TILE, TILE)
        cos = cos_ref[pl.ds(r0, TILE), :]
        sin = sin_ref[pl.ds(r0, TILE), :]
        qr_scr[pl.ds(r0, TILE), :] = _rotary(
            q_ref[0, pl.ds(r0, TILE), :].astype(F32), cos, sin).astype(BF16)
        kr_scr[pl.ds(r0, TILE), :] = (_rotary(
            k_ref[0, pl.ds(r0, TILE), :].astype(F32), cos, sin) * QK_SCALE).astype(BF16)
        return 0

    lax.fori_loop(0, nt, rot_body, 0)

    state_scr[...] = jnp.zeros_like(state_scr)
    lane = lax.broadcasted_iota(jnp.int32, (TILE, LANES), 1)
    first_head = lane < HEAD_DIM

    def chunk_body(i, _):
        r0 = pl.multiple_of(i * TILE, TILE)
        qc = qr_scr[pl.ds(r0, TILE), :]
        kc = kr_scr[pl.ds(r0, TILE), :]
        vc = v_ref[0, pl.ds(r0, TILE), :]
        state = state_scr[...]
        scores = _dot_nt(_stack_heads(qc), kc) * dec_scr[...]
        o = _unstack_heads(_dot(scores.astype(BF16), vc))
        o = o + _dot(qc, state.astype(BF16)) * query_decay
        kd = (kc.astype(F32) * key_decay).astype(BF16)
        state_scr[...] = chunk_decay * state + jnp.where(same_head, _dot_tn(kd, vc), 0.0)

        def head_mean(t):
            s0 = jnp.sum(jnp.where(first_head, t, 0.0), axis=-1, keepdims=True)
            s_all = jnp.sum(t, axis=-1, keepdims=True)
            return jnp.where(first_head, s0, s_all - s0) * (1.0 / HEAD_DIM)

        d = o - head_mean(o)
        var = head_mean(d * d)
        o_ref[0, pl.ds(r0, TILE), :] = (d * lax.rsqrt(var + GN_EPS) * gain_ref[...]).astype(o_ref.dtype)
        return 0

    lax.fori_loop(0, nt, chunk_body, 0)


def _ret(qkvg, log_g, cos, sin_signed, gain, batch, s_len):
    blk = (1, s_len, LANES)
    first = N_FOX_PAIRS
    tab = pl.BlockSpec((s_len, LANES), lambda b, p: (0, 0))
    return pl.pallas_call(
        _ret_kernel,
        grid=(batch, N_RET_PAIRS),
        in_specs=[
            pl.BlockSpec(memory_space=pltpu.SMEM),
            pl.BlockSpec(blk, lambda b, p: (b, 0, first + p)),
            pl.BlockSpec(blk, lambda b, p: (b, 0, N_PAIRS + first + p)),
            pl.BlockSpec(blk, lambda b, p: (b, 0, 2 * N_PAIRS + first + p)),
            tab, tab,
            pl.BlockSpec((1, LANES), lambda b, p: (0, p)),
        ],
        out_specs=pl.BlockSpec(blk, lambda b, p: (b, 0, p)),
        out_shape=jax.ShapeDtypeStruct((batch, s_len, N_RET_PAIRS * LANES), BF16),
        scratch_shapes=[
            pltpu.VMEM((s_len, LANES), BF16),
            pltpu.VMEM((s_len, LANES), BF16),
            pltpu.VMEM((PAIR * TILE, TILE), F32),
            pltpu.VMEM((LANES, LANES), F32),
        ],
        compiler_params=pltpu.CompilerParams(
            dimension_semantics=("parallel", "parallel"), vmem_limit_bytes=VMEM_LIMIT),
        name="retention",
    )(log_g, qkvg, qkvg, qkvg, cos, sin_signed, gain)


def _out_kernel(of_ref, or_ref, os_ref, g_ref, x_ref, w_ref, gain_ref, bias_ref, o_ref, y_scr,
                *, alpha):
    gate = g_ref[...].astype(F32)
    gate = gate * jax.nn.sigmoid(gate)
    n0 = 0
    for ref in (of_ref, or_ref, os_ref):
        n1 = n0 + ref.shape[1]
        y_scr[:, n0:n1] = (ref[...].astype(F32) * gate[:, n0:n1]).astype(BF16)
        n0 = n1
    h = alpha * x_ref[...] + _dot(y_scr[...], w_ref[...])
    mu = jnp.mean(h, axis=-1, keepdims=True)
    d = h - mu
    var = jnp.mean(d * d, axis=-1, keepdims=True)
    o_ref[...] = d * lax.rsqrt(var + LN_EPS) * gain_ref[...] + bias_ref[...]


def _out(o_fox, o_ret, o_sb, qkvg, x2d, w_out, gain, bias, alpha):
    m, d = x2d.shape
    row = lambda n: pl.BlockSpec((PROJ_TM, n), lambda i: (i, 0))
    vec = pl.BlockSpec((1, d), lambda i: (0, 0))
    return pl.pallas_call(
        functools.partial(_out_kernel, alpha=alpha),
        grid=(m // PROJ_TM,),
        in_specs=[
            row(o_fox.shape[1]), row(o_ret.shape[1]), row(o_sb.shape[1]),
            pl.BlockSpec((PROJ_TM, D_MIX), lambda i: (i, 3)),
            row(d),
            pl.BlockSpec((D_MIX, d), lambda i: (0, 0)),
            vec, vec,
        ],
        out_specs=row(d),
        out_shape=jax.ShapeDtypeStruct((m, d), F32),
        scratch_shapes=[pltpu.VMEM((PROJ_TM, D_MIX), BF16)],
        compiler_params=pltpu.CompilerParams(
            dimension_semantics=("parallel",), vmem_limit_bytes=VMEM_LIMIT),
        name="out_proj_ln",
    )(o_fox, o_ret, o_sb, qkvg, x2d, w_out, gain, bias)


def _rope_tables(s_len):
    half = HEAD_DIM // 2
    inv_freq = 1.0 / (ROPE_BASE ** (jnp.arange(half, dtype=F32) / half))
    ang = jnp.arange(s_len, dtype=F32)[:, None] * inv_freq[None, :]
    cos, sin = jnp.cos(ang), jnp.sin(ang)
    cos_t = jnp.tile(jnp.concatenate([cos, cos], axis=-1), (1, PAIR))
    sin_t = jnp.tile(jnp.concatenate([-sin, sin], axis=-1), (1, PAIR))
    return cos_t, sin_t


def kernel(x, w_in, b_fgate, ret_gn_gain, w_out, ln_gain, ln_bias):
    batch, s_len, d_model = x.shape
    depth = w_in.shape[0]
    alpha = (2 * depth) ** 0.25
    m = batch * s_len

    cos_t, sin_t = _rope_tables(s_len)
    log_g = jnp.log(1.0 - 2.0 ** (-5.0 - jnp.arange(N_RET_HEADS, dtype=F32)))

    x2d = x.reshape(m, d_model)
    for layer in range(depth):
        w_main = w_in[layer, :, :4 * D_MIX].astype(BF16)
        wf_t = jnp.zeros((F_ROWS, d_model), BF16).at[:N_FOX_HEADS].set(
            w_in[layer, :, 4 * D_MIX:].T.astype(BF16))
        b_f = jnp.zeros((F_ROWS, 1), F32).at[:N_FOX_HEADS, 0].set(b_fgate[layer])

        qkvg2d, f_t = _proj(x2d, w_main, wf_t)
        qkvg = qkvg2d.reshape(batch, s_len, 4 * D_MIX)
        o_fox = _fox(qkvg, f_t, b_f, batch, s_len)
        o_ret = _ret(qkvg, log_g, cos_t, sin_t, ret_gn_gain[layer].reshape(1, -1), batch, s_len)
        o_sb = _sb(qkvg, batch, s_len)
        x2d = _out(o_fox.reshape(m, -1), o_ret.reshape(m, -1), o_sb.reshape(m, -1), qkvg2d, x2d,
                   w_out[layer].astype(BF16), ln_gain[layer].reshape(1, -1),
                   ln_bias[layer].reshape(1, -1), alpha)
    return x2d.reshape(batch, s_len, d_model)
```

```python
import functools
import math

import jax
import jax.numpy as jnp
from jax import lax
from jax.experimental import pallas as pl
from jax.experimental.pallas import tpu as pltpu

F32 = jnp.float32
BF16 = jnp.bfloat16

HEAD_DIM = 64
N_FOX_HEADS = 6
N_RET_HEADS = 6
N_SB_HEADS = 4
N_HEADS = N_FOX_HEADS + N_RET_HEADS + N_SB_HEADS
D_MIX = N_HEADS * HEAD_DIM
ROPE_BASE = 10000.0
LN_EPS = 1e-5
GN_EPS = 1e-5
QK_SCALE = HEAD_DIM ** -0.5
LOG2E = math.log2(math.e)

LANES = 128
PAIR = LANES // HEAD_DIM
N_FOX_PAIRS = N_FOX_HEADS // PAIR
N_RET_PAIRS = N_RET_HEADS // PAIR
N_SB_PAIRS = N_SB_HEADS // PAIR
N_PAIRS = N_HEADS // PAIR
F_ROWS = 8

TILE = 256
PROJ_TM = 512
PROJ_TN = 512
VMEM_LIMIT = 48 * 1024 * 1024


def _dot(a, b):
    return jnp.dot(a, b, preferred_element_type=F32)


def _dot_nt(a, b):
    return lax.dot_general(a, b, (((1,), (1,)), ((), ())), preferred_element_type=F32)


def _split_bf16(x, parts):
    out = []
    for _ in range(parts - 1):
        p = x.astype(BF16)
        out.append(p)
        x = x - p.astype(F32)
    out.append(x.astype(BF16))
    return out


def _log_sigmoid_pair(z):
    lsp = jnp.minimum(z, 0.0) - jnp.log1p(jnp.exp(-jnp.abs(z)))
    return lsp, lsp - z


def _log2_sigmoid_pair(z2):
    sign = jnp.uint32(0x80000000)
    neg_abs = lax.bitcast_convert_type(lax.bitcast_convert_type(z2, jnp.uint32) | sign, F32)
    lsp = jnp.minimum(z2, 0.0) - jnp.log2(1.0 + jnp.exp2(neg_abs))
    return lsp, lsp - z2


def _stack_heads(q):
    lane = lax.broadcasted_iota(jnp.int32, q.shape, 1)
    zero = jnp.zeros_like(q)
    return jnp.concatenate([jnp.where(lane < HEAD_DIM, q, zero),
                            jnp.where(lane >= HEAD_DIM, q, zero)], axis=0)


def _unstack_heads(a):
    t = a.shape[0] // 2
    lane = lax.broadcasted_iota(jnp.int32, (t, LANES), 1)
    return jnp.where(lane < HEAD_DIM, a[:t], a[t:])


def _proj_kernel(x_ref, w_ref, wkf_ref, qscale_ref, qvg_ref, kt_ref, ft_ref):
    xb = x_ref[...].astype(BF16)
    for n in range(0, w_ref.shape[1], PROJ_TN):
        acc = _dot(xb, w_ref[:, n:n + PROJ_TN])
        if n < D_MIX:
            acc = acc * qscale_ref[:, n:n + PROJ_TN]
        qvg_ref[:, n:n + PROJ_TN] = acc.astype(BF16)
    kf = _dot_nt(wkf_ref[...], xb)
    ret0, ret1 = N_FOX_HEADS * HEAD_DIM, (N_FOX_HEADS + N_RET_HEADS) * HEAD_DIM
    kt_ref[:ret0] = kf[:ret0].astype(BF16)
    kt_ref[ret0:ret1] = (kf[ret0:ret1] * QK_SCALE).astype(BF16)
    kt_ref[ret1:] = kf[ret1:D_MIX].astype(BF16)
    ft_ref[...] = kf[D_MIX:]


def _proj(x2d, w_qvg, wkf_t, qscale):
    m, d = x2d.shape
    n = w_qvg.shape[1]
    return pl.pallas_call(
        _proj_kernel,
        grid=(m // PROJ_TM,),
        in_specs=[
            pl.BlockSpec((PROJ_TM, d), lambda i: (i, 0)),
            pl.BlockSpec((d, n), lambda i: (0, 0)),
            pl.BlockSpec((D_MIX + F_ROWS, d), lambda i: (0, 0)),
            pl.BlockSpec((1, D_MIX), lambda i: (0, 0)),
        ],
        out_specs=[
            pl.BlockSpec((PROJ_TM, n), lambda i: (i, 0)),
            pl.BlockSpec((D_MIX, PROJ_TM), lambda i: (0, i)),
            pl.BlockSpec((F_ROWS, PROJ_TM), lambda i: (0, i)),
        ],
        out_shape=[
            jax.ShapeDtypeStruct((m, n), BF16),
            jax.ShapeDtypeStruct((D_MIX, m), BF16),
            jax.ShapeDtypeStruct((F_ROWS, m), F32),
        ],
        compiler_params=pltpu.CompilerParams(
            dimension_semantics=("parallel",), vmem_limit_bytes=VMEM_LIMIT),
        name="proj",
    )(x2d, w_qvg, wkf_t, qscale)


def _fox_kernel(q_ref, kt_ref, v_ref, ft_ref, bf_ref, o_ref, c_scr, vaug_scr):
    s_len = q_ref.shape[1]
    nt = s_len // TILE
    pair = pl.program_id(1)

    f_all = ft_ref[...] + bf_ref[...]
    f_row = lax.broadcasted_iota(jnp.int32, f_all.shape, 0)
    f_sel = jnp.zeros_like(f_all)
    for h in range(PAIR):
        picked = jnp.sum(jnp.where(f_row == PAIR * pair + h, f_all, 0.0), axis=0, keepdims=True)
        f_sel = jnp.where(f_row == h, picked, f_sel)
    lsg, _ = _log_sigmoid_pair(f_sel)
    r = lax.broadcasted_iota(jnp.int32, (TILE, TILE), 0)
    c = lax.broadcasted_iota(jnp.int32, (TILE, TILE), 1)
    tri = jnp.where(r <= c, 1.0, 0.0).astype(BF16)
    tri3 = jnp.concatenate([tri, tri, tri], axis=0)
    carry = jnp.zeros((F_ROWS, 1), F32)
    for jb in range(nt):
        blk = lsg[:, jb * TILE:(jb + 1) * TILE]
        pieces = jnp.concatenate(_split_bf16(blk, 3), axis=1)
        cblk = _dot(pieces, tri3) + carry
        c_scr[:, jb * TILE:(jb + 1) * TILE] = cblk * LOG2E
        carry = cblk[:, TILE - 1:TILE]

    vaug_scr[:, :LANES] = v_ref[0]
    vaug_scr[:, LANES:] = jnp.ones((s_len, LANES), BF16)

    causal = c <= r
    for qi in range(nt):
        n = (qi + 1) * TILE
        qq = _stack_heads(q_ref[0, qi * TILE:n, :])
        s = _dot(qq, kt_ref[:, :n])
        ps = []
        for h in range(PAIR):
            sh = s[h * TILE:(h + 1) * TILE] - c_scr[h:h + 1, :n]
            diag = jnp.where(causal, sh[:, n - TILE:], -jnp.inf)
            sh = diag if qi == 0 else jnp.concatenate([sh[:, :n - TILE], diag], axis=1)
            m = jnp.max(sh, axis=-1, keepdims=True)
            ps.append(jnp.exp2(sh - m).astype(BF16))
        res = _dot(jnp.concatenate(ps, axis=0), vaug_scr[:n, :])
        o_ref[0, qi * TILE:n, :] = _unstack_heads(res[:, :LANES] / res[:, LANES:]).astype(o_ref.dtype)


def _fox(qvg, kt, f_t, b_f, batch, s_len):
    blk = (1, s_len, LANES)
    return pl.pallas_call(
        _fox_kernel,
        grid=(batch, N_FOX_PAIRS),
        in_specs=[
            pl.BlockSpec(blk, lambda b, p: (b, 0, p)),
            pl.BlockSpec((LANES, s_len), lambda b, p: (p, b)),
            pl.BlockSpec(blk, lambda b, p: (b, 0, N_PAIRS + p)),
            pl.BlockSpec((F_ROWS, s_len), lambda b, p: (0, b)),
            pl.BlockSpec((F_ROWS, 1), lambda b, p: (0, 0)),
        ],
        out_specs=pl.BlockSpec(blk, lambda b, p: (b, 0, p)),
        out_shape=jax.ShapeDtypeStruct((batch, s_len, N_FOX_PAIRS * LANES), BF16),
        scratch_shapes=[pltpu.VMEM((F_ROWS, s_len), F32),
                        pltpu.VMEM((s_len, 2 * LANES), BF16)],
        compiler_params=pltpu.CompilerParams(
            dimension_semantics=("parallel", "parallel"), vmem_limit_bytes=VMEM_LIMIT),
        name="fox",
    )(qvg, kt, qvg, f_t, b_f)


def _sb_kernel(q_ref, kt_ref, v_ref, o_ref):
    s_len = q_ref.shape[1]
    nt = s_len // TILE

    r = lax.broadcasted_iota(jnp.int32, (TILE, TILE), 0)
    c = lax.broadcasted_iota(jnp.int32, (TILE, TILE), 1)
    after = jnp.where(r > c, 1.0, 0.0).astype(BF16)
    after2 = jnp.concatenate([after, after], axis=0)
    strict1 = c < r
    strict = jnp.concatenate([strict1, strict1], axis=0)

    for qi in range(nt):
        n = (qi + 1) * TILE
        qq = _stack_heads(q_ref[0, qi * TILE:n, :])
        z2 = _dot(qq, kt_ref[:, :n])
        lsp, lsn = _log2_sigmoid_pair(z2)
        tail = jnp.zeros((PAIR * TILE, 1), F32)
        weights = [None] * (qi + 1)
        for j in range(qi, -1, -1):
            lsn_j = lsn[:, j * TILE:(j + 1) * TILE]
            if j == qi:
                lsn_j = jnp.where(strict, lsn_j, 0.0)
            hi, lo = _split_bf16(lsn_j, 2)
            cum = _dot(jnp.concatenate([hi, lo], axis=1), after2)
            a = jnp.exp2(lsp[:, j * TILE:(j + 1) * TILE] + cum + tail)
            if j == qi:
                a = jnp.where(strict, a, 0.0)
            weights[j] = a.astype(BF16)
            tail = tail + jnp.sum(lsn_j, axis=-1, keepdims=True)
        acc = _dot(jnp.concatenate(weights, axis=1), v_ref[0, :n, :])
        o_ref[0, qi * TILE:n, :] = _unstack_heads(acc).astype(o_ref.dtype)


def _sb(qvg, kt, batch, s_len):
    blk = (1, s_len, LANES)
    first = N_FOX_PAIRS + N_RET_PAIRS
    return pl.pallas_call(
        _sb_kernel,
        grid=(batch, N_SB_PAIRS),
        in_specs=[
            pl.BlockSpec(blk, lambda b, p: (b, 0, first + p)),
            pl.BlockSpec((LANES, s_len), lambda b, p: (first + p, b)),
            pl.BlockSpec(blk, lambda b, p: (b, 0, N_PAIRS + first + p)),
        ],
        out_specs=pl.BlockSpec(blk, lambda b, p: (b, 0, p)),
        out_shape=jax.ShapeDtypeStruct((batch, s_len, N_SB_PAIRS * LANES), BF16),
        compiler_params=pltpu.CompilerParams(
            dimension_semantics=("parallel", "parallel"), vmem_limit_bytes=VMEM_LIMIT),
        name="stick_breaking",
    )(qvg, kt, qvg)


def _rotary(t, cos, sin_signed):
    lane = lax.broadcasted_iota(jnp.int32, t.shape, 1)
    half = HEAD_DIM // 2
    partner = jnp.where((lane & (HEAD_DIM - 1)) < half,
                        pltpu.roll(t, LANES - half, 1),
                        pltpu.roll(t, half, 1))
    return t * cos + partner * sin_signed


def _rotary_t(t, cos_t, sin_signed_t):
    half = HEAD_DIM // 2
    blocks = [t[i * half:(i + 1) * half] for i in range(LANES // half)]
    partner = jnp.concatenate([blocks[i ^ 1] for i in range(len(blocks))], axis=0)
    return t * cos_t + partner * sin_signed_t


def _ret_kernel(lg_ref, q_ref, kt_ref, v_ref, cos_ref, sin_ref, cost_ref, sint_ref, gain_ref, o_ref,
                qr_scr, kr_scr, dec_scr, state_scr):
    s_len = q_ref.shape[1]
    nt = s_len // TILE
    pair = pl.program_id(1)
    lg0 = lg_ref[PAIR * pair]
    lg1 = lg_ref[PAIR * pair + 1]

    lane_row = lax.broadcasted_iota(jnp.int32, (1, LANES), 1)
    lg_lane = jnp.where(lane_row < HEAD_DIM, lg0, lg1)
    idx = lax.broadcasted_iota(jnp.int32, (TILE, LANES), 0).astype(F32)
    query_decay = jnp.exp((idx + 1.0) * lg_lane)
    chunk_decay = jnp.exp(float(TILE) * lg_lane)
    sub_row = lax.broadcasted_iota(jnp.int32, (LANES, TILE), 0)
    pos_t = lax.broadcasted_iota(jnp.int32, (LANES, TILE), 1).astype(F32)
    key_decay_t = jnp.exp((TILE - 1.0 - pos_t) * jnp.where(sub_row < HEAD_DIM, lg0, lg1))

    r = lax.broadcasted_iota(jnp.int32, (TILE, TILE), 0)
    c = lax.broadcasted_iota(jnp.int32, (TILE, TILE), 1)
    diff = jnp.maximum(r - c, 0).astype(F32)
    for h, lg in enumerate((lg0, lg1)):
        dec_scr[h * TILE:(h + 1) * TILE, :] = jnp.where(r >= c, jnp.exp(diff * lg), 0.0)

    rr = lax.broadcasted_iota(jnp.int32, (LANES, LANES), 0)
    cc = lax.broadcasted_iota(jnp.int32, (LANES, LANES), 1)
    same_head = (rr < HEAD_DIM) == (cc < HEAD_DIM)

    for i in range(nt):
        rows = slice(i * TILE, (i + 1) * TILE)
        qr_scr[rows, :] = _rotary(q_ref[0, rows, :].astype(F32),
                                  cos_ref[rows, :], sin_ref[rows, :]).astype(BF16)
        kr_scr[i] = _rotary_t(kt_ref[:, rows].astype(F32), cost_ref[:, rows], sint_ref[:, rows])

    state_scr[...] = jnp.zeros_like(state_scr)
    lane = lax.broadcasted_iota(jnp.int32, (TILE, LANES), 1)
    first_head = lane < HEAD_DIM

    def chunk_body(i, _):
        r0 = pl.multiple_of(i * TILE, TILE)
        qc = qr_scr[pl.ds(r0, TILE), :]
        kt = kr_scr[i]
        vc = v_ref[0, pl.ds(r0, TILE), :]
        state = state_scr[...]
        scores = _dot(_stack_heads(qc), kt.astype(BF16)) * dec_scr[...]
        o = _unstack_heads(_dot(scores.astype(BF16), vc))
        o = o + _dot(qc, state.astype(BF16)) * query_decay
        kd = (kt * key_decay_t).astype(BF16)
        state_scr[...] = chunk_decay * state + jnp.where(same_head, _dot(kd, vc), 0.0)

        def head_mean(t):
            s0 = jnp.sum(jnp.where(first_head, t, 0.0), axis=-1, keepdims=True)
            s_all = jnp.sum(t, axis=-1, keepdims=True)
            return jnp.where(first_head, s0, s_all - s0) * (1.0 / HEAD_DIM)

        d = o - head_mean(o)
        var = head_mean(d * d)
        o_ref[0, pl.ds(r0, TILE), :] = (d * lax.rsqrt(var + GN_EPS) * gain_ref[...]).astype(o_ref.dtype)
        return 0

    lax.fori_loop(0, nt, chunk_body, 0)


def _ret(qvg, kt, log_g, tables, gain, batch, s_len):
    blk = (1, s_len, LANES)
    first = N_FOX_PAIRS
    nt = s_len // TILE
    tab = pl.BlockSpec((s_len, LANES), lambda b, p: (0, 0))
    tab_t = pl.BlockSpec((LANES, s_len), lambda b, p: (0, 0))
    return pl.pallas_call(
        _ret_kernel,
        grid=(batch, N_RET_PAIRS),
        in_specs=[
            pl.BlockSpec(memory_space=pltpu.SMEM),
            pl.BlockSpec(blk, lambda b, p: (b, 0, first + p)),
            pl.BlockSpec((LANES, s_len), lambda b, p: (first + p, b)),
            pl.BlockSpec(blk, lambda b, p: (b, 0, N_PAIRS + first + p)),
            tab, tab, tab_t, tab_t,
            pl.BlockSpec((1, LANES), lambda b, p: (0, p)),
        ],
        out_specs=pl.BlockSpec(blk, lambda b, p: (b, 0, p)),
        out_shape=jax.ShapeDtypeStruct((batch, s_len, N_RET_PAIRS * LANES), BF16),
        scratch_shapes=[
            pltpu.VMEM((s_len, LANES), BF16),
            pltpu.VMEM((nt, LANES, TILE), F32),
            pltpu.VMEM((PAIR * TILE, TILE), F32),
            pltpu.VMEM((LANES, LANES), F32),
        ],
        compiler_params=pltpu.CompilerParams(
            dimension_semantics=("parallel", "parallel"), vmem_limit_bytes=VMEM_LIMIT),
        name="retention",
    )(log_g, qvg, kt, qvg, *tables, gain)


def _out_kernel(of_ref, or_ref, os_ref, g_ref, x_ref, w_ref, gain_ref, bias_ref, o_ref, y_scr,
                *, alpha):
    gate = g_ref[...].astype(F32)
    gate = gate * jax.nn.sigmoid(gate)
    n0 = 0
    for ref in (of_ref, or_ref, os_ref):
        n1 = n0 + ref.shape[1]
        y_scr[:, n0:n1] = (ref[...].astype(F32) * gate[:, n0:n1]).astype(BF16)
        n0 = n1
    h = alpha * x_ref[...] + _dot(y_scr[...], w_ref[...])
    mu = jnp.mean(h, axis=-1, keepdims=True)
    d = h - mu
    var = jnp.mean(d * d, axis=-1, keepdims=True)
    o_ref[...] = d * lax.rsqrt(var + LN_EPS) * gain_ref[...] + bias_ref[...]


def _out(o_fox, o_ret, o_sb, qvg, x2d, w_out, gain, bias, alpha):
    m, d = x2d.shape
    row = lambda n: pl.BlockSpec((PROJ_TM, n), lambda i: (i, 0))
    vec = pl.BlockSpec((1, d), lambda i: (0, 0))
    return pl.pallas_call(
        functools.partial(_out_kernel, alpha=alpha),
        grid=(m // PROJ_TM,),
        in_specs=[
            row(o_fox.shape[1]), row(o_ret.shape[1]), row(o_sb.shape[1]),
            pl.BlockSpec((PROJ_TM, D_MIX), lambda i: (i, 2)),
            row(d),
            pl.BlockSpec((D_MIX, d), lambda i: (0, 0)),
            vec, vec,
        ],
        out_specs=row(d),
        out_shape=jax.ShapeDtypeStruct((m, d), F32),
        scratch_shapes=[pltpu.VMEM((PROJ_TM, D_MIX), BF16)],
        compiler_params=pltpu.CompilerParams(
            dimension_semantics=("parallel",), vmem_limit_bytes=VMEM_LIMIT),
        name="out_proj_ln",
    )(o_fox, o_ret, o_sb, qvg, x2d, w_out, gain, bias)


def _rope_tables(s_len):
    half = HEAD_DIM // 2
    inv_freq = 1.0 / (ROPE_BASE ** (jnp.arange(half, dtype=F32) / half))
    ang = jnp.arange(s_len, dtype=F32)[:, None] * inv_freq[None, :]
    cos, sin = jnp.cos(ang), jnp.sin(ang)
    cos_p = jnp.tile(jnp.concatenate([cos, cos], axis=-1), (1, PAIR))
    sin_p = jnp.tile(jnp.concatenate([-sin, sin], axis=-1), (1, PAIR))
    return cos_p, sin_p, cos_p.T, sin_p.T


def kernel(x, w_in, b_fgate, ret_gn_gain, w_out, ln_gain, ln_bias):
    batch, s_len, d_model = x.shape
    depth = w_in.shape[0]
    alpha = (2 * depth) ** 0.25
    m = batch * s_len

    tables = _rope_tables(s_len)
    log_g = jnp.log(1.0 - 2.0 ** (-5.0 - jnp.arange(N_RET_HEADS, dtype=F32)))
    head_scale = ([QK_SCALE * LOG2E] * N_FOX_HEADS + [1.0] * N_RET_HEADS
                  + [QK_SCALE * LOG2E] * N_SB_HEADS)
    qscale = jnp.repeat(jnp.asarray(head_scale, F32), HEAD_DIM).reshape(1, D_MIX)

    x2d = x.reshape(m, d_model)
    for layer in range(depth):
        w = w_in[layer]
        w_qvg = jnp.concatenate([w[:, :D_MIX], w[:, 2 * D_MIX:4 * D_MIX]], axis=1).astype(BF16)
        wkf_t = jnp.zeros((D_MIX + F_ROWS, d_model), BF16)
        wkf_t = wkf_t.at[:D_MIX].set(w[:, D_MIX:2 * D_MIX].T.astype(BF16))
        wkf_t = wkf_t.at[D_MIX:D_MIX + N_FOX_HEADS].set(w[:, 4 * D_MIX:].T.astype(BF16))
        b_f = jnp.zeros((F_ROWS, 1), F32).at[:N_FOX_HEADS, 0].set(b_fgate[layer])

        qvg2d, kt, f_t = _proj(x2d, w_qvg, wkf_t, qscale)
        qvg = qvg2d.reshape(batch, s_len, 3 * D_MIX)
        o_fox = _fox(qvg, kt, f_t, b_f, batch, s_len)
        o_ret = _ret(qvg, kt, log_g, tables, ret_gn_gain[layer].reshape(1, -1), batch, s_len)
        o_sb = _sb(qvg, kt, batch, s_len)
        x2d = _out(o_fox.reshape(m, -1), o_ret.reshape(m, -1), o_sb.reshape(m, -1), qvg2d, x2d,
                   w_out[layer].astype(BF16), ln_gain[layer].reshape(1, -1),
                   ln_bias[layer].reshape(1, -1), alpha)
    return x2d.reshape(batch, s_len, d_model)
```

```python
import functools
import math

import jax
import jax.numpy as jnp
from jax import lax
from jax.experimental import pallas as pl
from jax.experimental.pallas import tpu as pltpu

F32 = jnp.float32
BF16 = jnp.bfloat16

HEAD_DIM = 64
N_FOX_HEADS = 6
N_RET_HEADS = 6
N_SB_HEADS = 4
N_HEADS = N_FOX_HEADS + N_RET_HEADS + N_SB_HEADS
D_MIX = N_HEADS * HEAD_DIM
ROPE_BASE = 10000.0
LN_EPS = 1e-5
GN_EPS = 1e-5
QK_SCALE = HEAD_DIM ** -0.5
LOG2E = math.log2(math.e)

LANES = 128
PAIR = LANES // HEAD_DIM
N_FOX_PAIRS = N_FOX_HEADS // PAIR
N_RET_PAIRS = N_RET_HEADS // PAIR
N_SB_PAIRS = N_SB_HEADS // PAIR
N_PAIRS = N_HEADS // PAIR
F_ROWS = 8

TILE = 256
PROJ_TM = 512
PROJ_TN = 512
VMEM_LIMIT = 48 * 1024 * 1024


def _dot(a, b):
    return jnp.dot(a, b, preferred_element_type=F32)


def _dot_nt(a, b):
    return lax.dot_general(a, b, (((1,), (1,)), ((), ())), preferred_element_type=F32)


def _split_bf16(x, parts):
    out = []
    for _ in range(parts - 1):
        p = x.astype(BF16)
        out.append(p)
        x = x - p.astype(F32)
    out.append(x.astype(BF16))
    return out


def _log_sigmoid(z):
    return jnp.minimum(z, 0.0) - jnp.log1p(jnp.exp(-jnp.abs(z)))


def _log2_sigmoid_pair(z2):
    lsp = jnp.minimum(z2, 0.0) - jnp.log2(1.0 + jnp.exp2(-jnp.abs(z2)))
    return lsp, lsp - z2


def _stack_heads(q):
    lane = lax.broadcasted_iota(jnp.int32, q.shape, 1)
    zero = jnp.zeros_like(q)
    return jnp.concatenate([jnp.where(lane < HEAD_DIM, q, zero),
                            jnp.where(lane >= HEAD_DIM, q, zero)], axis=0)


def _unstack_heads(a):
    t = a.shape[0] // 2
    lane = lax.broadcasted_iota(jnp.int32, (t, LANES), 1)
    return jnp.where(lane < HEAD_DIM, a[:t], a[t:])


def _gated(o, g_ref, rows):
    g = g_ref[0, rows, :].astype(F32)
    return (o * (g * jax.nn.sigmoid(g))).astype(BF16)


def _proj_kernel(x_ref, w_ref, wkf_ref, qscale_ref, qvg_ref, kt_ref, ft_ref):
    xb = x_ref[...].astype(BF16)
    for n in range(0, w_ref.shape[1], PROJ_TN):
        acc = _dot(xb, w_ref[:, n:n + PROJ_TN])
        if n < D_MIX:
            acc = acc * qscale_ref[:, n:n + PROJ_TN]
        qvg_ref[:, n:n + PROJ_TN] = acc.astype(BF16)
    kf = _dot_nt(wkf_ref[...], xb)
    ret0, ret1 = N_FOX_HEADS * HEAD_DIM, (N_FOX_HEADS + N_RET_HEADS) * HEAD_DIM
    kt_ref[:ret0] = kf[:ret0].astype(BF16)
    kt_ref[ret0:ret1] = (kf[ret0:ret1] * QK_SCALE).astype(BF16)
    kt_ref[ret1:] = kf[ret1:D_MIX].astype(BF16)
    ft_ref[...] = kf[D_MIX:]


def _proj(x2d, w_qvg, wkf_t, qscale):
    m, d = x2d.shape
    n = w_qvg.shape[1]
    return pl.pallas_call(
        _proj_kernel,
        grid=(m // PROJ_TM,),
        in_specs=[
            pl.BlockSpec((PROJ_TM, d), lambda i: (i, 0)),
            pl.BlockSpec((d, n), lambda i: (0, 0)),
            pl.BlockSpec((D_MIX + F_ROWS, d), lambda i: (0, 0)),
            pl.BlockSpec((1, D_MIX), lambda i: (0, 0)),
        ],
        out_specs=[
            pl.BlockSpec((PROJ_TM, n), lambda i: (i, 0)),
            pl.BlockSpec((D_MIX, PROJ_TM), lambda i: (0, i)),
            pl.BlockSpec((F_ROWS, PROJ_TM), lambda i: (0, i)),
        ],
        out_shape=[
            jax.ShapeDtypeStruct((m, n), BF16),
            jax.ShapeDtypeStruct((D_MIX, m), BF16),
            jax.ShapeDtypeStruct((F_ROWS, m), F32),
        ],
        compiler_params=pltpu.CompilerParams(
            dimension_semantics=("parallel",), vmem_limit_bytes=VMEM_LIMIT),
        name="proj",
    )(x2d, w_qvg, wkf_t, qscale)


def _fox_body(pair, q_ref, kt_ref, v_ref, g_ref, ft_ref, bf_ref, y_ref, c_scr, vaug_scr):
    s_len = q_ref.shape[1]
    nt = s_len // TILE

    f_all = ft_ref[...] + bf_ref[...]
    f_row = lax.broadcasted_iota(jnp.int32, f_all.shape, 0)
    f_sel = jnp.zeros_like(f_all)
    for h in range(PAIR):
        picked = jnp.sum(jnp.where(f_row == PAIR * pair + h, f_all, 0.0), axis=0, keepdims=True)
        f_sel = jnp.where(f_row == h, picked, f_sel)
    lsg = _log_sigmoid(f_sel)
    r = lax.broadcasted_iota(jnp.int32, (TILE, TILE), 0)
    c = lax.broadcasted_iota(jnp.int32, (TILE, TILE), 1)
    tri = jnp.where(r <= c, 1.0, 0.0).astype(BF16)
    tri3 = jnp.concatenate([tri, tri, tri], axis=0)
    carry = jnp.zeros((F_ROWS, 1), F32)
    for jb in range(nt):
        blk = lsg[:, jb * TILE:(jb + 1) * TILE]
        pieces = jnp.concatenate(_split_bf16(blk, 3), axis=1)
        cblk = _dot(pieces, tri3) + carry
        c_scr[:, jb * TILE:(jb + 1) * TILE] = cblk * LOG2E
        carry = cblk[:, TILE - 1:TILE]

    vaug_scr[:, :LANES] = v_ref[0]
    vaug_scr[:, LANES:] = jnp.ones((s_len, LANES), BF16)

    causal = c <= r
    for qi in range(nt):
        n = (qi + 1) * TILE
        rows = slice(qi * TILE, n)
        qq = _stack_heads(q_ref[0, rows, :])
        s = _dot(qq, kt_ref[:, :n])
        ps = []
        for h in range(PAIR):
            sh = s[h * TILE:(h + 1) * TILE] - c_scr[h:h + 1, :n]
            diag = jnp.where(causal, sh[:, n - TILE:], -jnp.inf)
            sh = diag if qi == 0 else jnp.concatenate([sh[:, :n - TILE], diag], axis=1)
            m = jnp.max(sh, axis=-1, keepdims=True)
            ps.append(jnp.exp2(sh - m).astype(BF16))
        res = _dot(jnp.concatenate(ps, axis=0), vaug_scr[:n, :])
        y_ref[0, rows, :] = _gated(_unstack_heads(res[:, :LANES] / res[:, LANES:]), g_ref, rows)


def _sb_body(q_ref, kt_ref, v_ref, g_ref, y_ref):
    s_len = q_ref.shape[1]
    nt = s_len // TILE

    r = lax.broadcasted_iota(jnp.int32, (TILE, TILE), 0)
    c = lax.broadcasted_iota(jnp.int32, (TILE, TILE), 1)
    after = jnp.where(r > c, 1.0, 0.0).astype(BF16)
    after2 = jnp.concatenate([after, after], axis=0)
    strict1 = c < r
    strict = jnp.concatenate([strict1, strict1], axis=0)

    for qi in range(nt):
        n = (qi + 1) * TILE
        rows = slice(qi * TILE, n)
        qq = _stack_heads(q_ref[0, rows, :])
        z2 = _dot(qq, kt_ref[:, :n])
        lsp, lsn = _log2_sigmoid_pair(z2)
        tail = jnp.zeros((PAIR * TILE, 1), F32)
        weights = [None] * (qi + 1)
        for j in range(qi, -1, -1):
            lsn_j = lsn[:, j * TILE:(j + 1) * TILE]
            if j == qi:
                lsn_j = jnp.where(strict, lsn_j, 0.0)
            hi, lo = _split_bf16(lsn_j, 2)
            cum = _dot(jnp.concatenate([hi, lo], axis=1), after2)
            a = jnp.exp2(lsp[:, j * TILE:(j + 1) * TILE] + cum + tail)
            if j == qi:
                a = jnp.where(strict, a, 0.0)
            weights[j] = a.astype(BF16)
            tail = tail + jnp.sum(lsn_j, axis=-1, keepdims=True)
        acc = _dot(jnp.concatenate(weights, axis=1), v_ref[0, :n, :])
        y_ref[0, rows, :] = _gated(_unstack_heads(acc), g_ref, rows)


def _rotary(t, cos, sin_signed):
    lane = lax.broadcasted_iota(jnp.int32, t.shape, 1)
    half = HEAD_DIM // 2
    partner = jnp.where((lane & (HEAD_DIM - 1)) < half,
                        pltpu.roll(t, LANES - half, 1),
                        pltpu.roll(t, half, 1))
    return t * cos + partner * sin_signed


def _rotary_t(t, cos_t, sin_signed_t):
    half = HEAD_DIM // 2
    blocks = [t[i * half:(i + 1) * half] for i in range(LANES // half)]
    partner = jnp.concatenate([blocks[i ^ 1] for i in range(len(blocks))], axis=0)
    return t * cos_t + partner * sin_signed_t


def _ret_body(pair, lg_ref, q_ref, kt_ref, v_ref, g_ref, cos_ref, sin_ref, cost_ref, sint_ref,
              gain_ref, y_ref):
    s_len = q_ref.shape[1]
    nt = s_len // TILE
    lg0 = lg_ref[PAIR * pair]
    lg1 = lg_ref[PAIR * pair + 1]

    lane_row = lax.broadcasted_iota(jnp.int32, (1, LANES), 1)
    lg_lane = jnp.where(lane_row < HEAD_DIM, lg0, lg1)
    idx = lax.broadcasted_iota(jnp.int32, (TILE, LANES), 0).astype(F32)
    query_decay = jnp.exp((idx + 1.0) * lg_lane)
    chunk_decay = jnp.exp(float(TILE) * lg_lane)
    sub_row = lax.broadcasted_iota(jnp.int32, (LANES, TILE), 0)
    pos_t = lax.broadcasted_iota(jnp.int32, (LANES, TILE), 1).astype(F32)
    key_decay_t = jnp.exp((TILE - 1.0 - pos_t) * jnp.where(sub_row < HEAD_DIM, lg0, lg1))

    r = lax.broadcasted_iota(jnp.int32, (TILE, TILE), 0)
    c = lax.broadcasted_iota(jnp.int32, (TILE, TILE), 1)
    diff = jnp.maximum(r - c, 0).astype(F32)
    intra_decay = jnp.concatenate(
        [jnp.where(r >= c, jnp.exp(diff * lg), 0.0) for lg in (lg0, lg1)], axis=0)

    rr = lax.broadcasted_iota(jnp.int32, (LANES, LANES), 0)
    cc = lax.broadcasted_iota(jnp.int32, (LANES, LANES), 1)
    same_head = (rr < HEAD_DIM) == (cc < HEAD_DIM)
    lane = lax.broadcasted_iota(jnp.int32, (TILE, LANES), 1)
    first_head = lane < HEAD_DIM

    def head_mean(t):
        s0 = jnp.sum(jnp.where(first_head, t, 0.0), axis=-1, keepdims=True)
        s_all = jnp.sum(t, axis=-1, keepdims=True)
        return jnp.where(first_head, s0, s_all - s0) * (1.0 / HEAD_DIM)

    state = jnp.zeros((LANES, LANES), F32)
    for i in range(nt):
        rows = slice(i * TILE, (i + 1) * TILE)
        qc = _rotary(q_ref[0, rows, :].astype(F32), cos_ref[rows, :], sin_ref[rows, :]).astype(BF16)
        kt = _rotary_t(kt_ref[:, rows].astype(F32), cost_ref[:, rows], sint_ref[:, rows])
        vc = v_ref[0, rows, :]
        scores = _dot(_stack_heads(qc), kt.astype(BF16)) * intra_decay
        o = _unstack_heads(_dot(scores.astype(BF16), vc))
        o = o + _dot(qc, state.astype(BF16)) * query_decay
        kd = (kt * key_decay_t).astype(BF16)
        state = chunk_decay * state + jnp.where(same_head, _dot(kd, vc), 0.0)

        d = o - head_mean(o)
        var = head_mean(d * d)
        y_ref[0, rows, :] = _gated(d * lax.rsqrt(var + GN_EPS) * gain_ref[...], g_ref, rows)


def _mixer_kernel(lg_ref, q_ref, kt_ref, v_ref, g_ref, ft_ref, bf_ref, cos_ref, sin_ref, cost_ref,
                  sint_ref, gain_ref, y_ref, c_scr, vaug_scr):
    pair = pl.program_id(1)
    ret0 = N_FOX_PAIRS
    sb0 = N_FOX_PAIRS + N_RET_PAIRS

    @pl.when(pair < ret0)
    def _():
        _fox_body(pair, q_ref, kt_ref, v_ref, g_ref, ft_ref, bf_ref, y_ref, c_scr, vaug_scr)

    @pl.when(jnp.logical_and(pair >= ret0, pair < sb0))
    def _():
        _ret_body(pair - ret0, lg_ref, q_ref, kt_ref, v_ref, g_ref, cos_ref, sin_ref, cost_ref,
                  sint_ref, gain_ref, y_ref)

    @pl.when(pair >= sb0)
    def _():
        _sb_body(q_ref, kt_ref, v_ref, g_ref, y_ref)


def _mixer(qvg, kt, f_t, b_f, log_g, tables, gain, batch, s_len):
    blk = (1, s_len, LANES)
    tab = pl.BlockSpec((s_len, LANES), lambda b, p: (0, 0))
    tab_t = pl.BlockSpec((LANES, s_len), lambda b, p: (0, 0))
    return pl.pallas_call(
        _mixer_kernel,
        grid=(batch, N_PAIRS),
        in_specs=[
            pl.BlockSpec(memory_space=pltpu.SMEM),
            pl.BlockSpec(blk, lambda b, p: (b, 0, p)),
            pl.BlockSpec((LANES, s_len), lambda b, p: (p, b)),
            pl.BlockSpec(blk, lambda b, p: (b, 0, N_PAIRS + p)),
            pl.BlockSpec(blk, lambda b, p: (b, 0, 2 * N_PAIRS + p)),
            pl.BlockSpec((F_ROWS, s_len), lambda b, p: (0, b)),
            pl.BlockSpec((F_ROWS, 1), lambda b, p: (0, 0)),
            tab, tab, tab_t, tab_t,
            pl.BlockSpec((1, LANES),
                         lambda b, p: (0, jnp.clip(p - N_FOX_PAIRS, 0, N_RET_PAIRS - 1))),
        ],
        out_specs=pl.BlockSpec(blk, lambda b, p: (b, 0, p)),
        out_shape=jax.ShapeDtypeStruct((batch, s_len, D_MIX), BF16),
        scratch_shapes=[pltpu.VMEM((F_ROWS, s_len), F32),
                        pltpu.VMEM((s_len, 2 * LANES), BF16)],
        compiler_params=pltpu.CompilerParams(
            dimension_semantics=("parallel", "parallel"), vmem_limit_bytes=VMEM_LIMIT),
        name="mixer",
    )(log_g, qvg, kt, qvg, qvg, f_t, b_f, *tables, gain)


def _out_kernel(y_ref, x_ref, w_ref, gain_ref, bias_ref, o_ref, *, alpha):
    h = alpha * x_ref[...] + _dot(y_ref[...], w_ref[...])
    mu = jnp.mean(h, axis=-1, keepdims=True)
    d = h - mu
    var = jnp.mean(d * d, axis=-1, keepdims=True)
    o_ref[...] = d * lax.rsqrt(var + LN_EPS) * gain_ref[...] + bias_ref[...]


def _out(y2d, x2d, w_out, gain, bias, alpha):
    m, d = x2d.shape
    row = lambda n: pl.BlockSpec((PROJ_TM, n), lambda i: (i, 0))
    vec = pl.BlockSpec((1, d), lambda i: (0, 0))
    return pl.pallas_call(
        functools.partial(_out_kernel, alpha=alpha),
        grid=(m // PROJ_TM,),
        in_specs=[row(D_MIX), row(d), pl.BlockSpec((D_MIX, d), lambda i: (0, 0)), vec, vec],
        out_specs=row(d),
        out_shape=jax.ShapeDtypeStruct((m, d), F32),
        compiler_params=pltpu.CompilerParams(
            dimension_semantics=("parallel",), vmem_limit_bytes=VMEM_LIMIT),
        name="out_proj_ln",
    )(y2d, x2d, w_out, gain, bias)


def _rope_tables(s_len):
    half = HEAD_DIM // 2
    inv_freq = 1.0 / (ROPE_BASE ** (jnp.arange(half, dtype=F32) / half))
    ang = jnp.arange(s_len, dtype=F32)[:, None] * inv_freq[None, :]
    cos, sin = jnp.cos(ang), jnp.sin(ang)
    cos_p = jnp.tile(jnp.concatenate([cos, cos], axis=-1), (1, PAIR))
    sin_p = jnp.tile(jnp.concatenate([-sin, sin], axis=-1), (1, PAIR))
    return cos_p, sin_p, cos_p.T, sin_p.T


def kernel(x, w_in, b_fgate, ret_gn_gain, w_out, ln_gain, ln_bias):
    batch, s_len, d_model = x.shape
    depth = w_in.shape[0]
    alpha = (2 * depth) ** 0.25
    m = batch * s_len

    tables = _rope_tables(s_len)
    log_g = jnp.log(1.0 - 2.0 ** (-5.0 - jnp.arange(N_RET_HEADS, dtype=F32)))
    head_scale = ([QK_SCALE * LOG2E] * N_FOX_HEADS + [1.0] * N_RET_HEADS
                  + [QK_SCALE * LOG2E] * N_SB_HEADS)
    qscale = jnp.repeat(jnp.asarray(head_scale, F32), HEAD_DIM).reshape(1, D_MIX)

    x2d = x.reshape(m, d_model)
    for layer in range(depth):
        w = w_in[layer]
        w_qvg = jnp.concatenate([w[:, :D_MIX], w[:, 2 * D_MIX:4 * D_MIX]], axis=1).astype(BF16)
        wkf_t = jnp.zeros((D_MIX + F_ROWS, d_model), BF16)
        wkf_t = wkf_t.at[:D_MIX].set(w[:, D_MIX:2 * D_MIX].T.astype(BF16))
        wkf_t = wkf_t.at[D_MIX:D_MIX + N_FOX_HEADS].set(w[:, 4 * D_MIX:].T.astype(BF16))
        b_f = jnp.zeros((F_ROWS, 1), F32).at[:N_FOX_HEADS, 0].set(b_fgate[layer])

        qvg2d, kt, f_t = _proj(x2d, w_qvg, wkf_t, qscale)
        qvg = qvg2d.reshape(batch, s_len, 3 * D_MIX)
        y = _mixer(qvg, kt, f_t, b_f, log_g, tables, ret_gn_gain[layer].reshape(1, -1),
                   batch, s_len)
        x2d = _out(y.reshape(m, D_MIX), x2d, w_out[layer].astype(BF16),
                   ln_gain[layer].reshape(1, -1), ln_bias[layer].reshape(1, -1), alpha)
    return x2d.reshape(batch, s_len, d_model)
```

```python
import functools
import math

import jax
import jax.numpy as jnp
from jax import lax
from jax.experimental import pallas as pl
from jax.experimental.pallas import tpu as pltpu

F32 = jnp.float32
BF16 = jnp.bfloat16

HEAD_DIM = 64
N_FOX_HEADS = 6
N_RET_HEADS = 6
N_SB_HEADS = 4
N_HEADS = N_FOX_HEADS + N_RET_HEADS + N_SB_HEADS
D_MIX = N_HEADS * HEAD_DIM
ROPE_BASE = 10000.0
LN_EPS = 1e-5
GN_EPS = 1e-5
QK_SCALE = HEAD_DIM ** -0.5
LOG2E = math.log2(math.e)

LANES = 128
PAIR = LANES // HEAD_DIM
N_FOX_PAIRS = N_FOX_HEADS // PAIR
N_RET_PAIRS = N_RET_HEADS // PAIR
N_SB_PAIRS = N_SB_HEADS // PAIR
N_PAIRS = N_HEADS // PAIR
F_ROWS = 8

TILE = 256
PROJ_TM = 512
PROJ_TN = 512
VMEM_LIMIT = 48 * 1024 * 1024


def _dot(a, b):
    return jnp.dot(a, b, preferred_element_type=F32)


def _dot_nt(a, b):
    return lax.dot_general(a, b, (((1,), (1,)), ((), ())), preferred_element_type=F32)


def _split_bf16(x, parts):
    out = []
    for _ in range(parts - 1):
        p = x.astype(BF16)
        out.append(p)
        x = x - p.astype(F32)
    out.append(x.astype(BF16))
    return out


def _log_sigmoid(z):
    return jnp.minimum(z, 0.0) - jnp.log1p(jnp.exp(-jnp.abs(z)))


def _log2_one_minus_sigmoid(z2):
    nz = -z2
    return jnp.minimum(nz, 0.0) - jnp.log2(1.0 + jnp.exp2(jnp.minimum(z2, nz)))


def _stack_heads(q):
    lane = lax.broadcasted_iota(jnp.int32, q.shape, 1)
    zero = jnp.zeros_like(q)
    return jnp.concatenate([jnp.where(lane < HEAD_DIM, q, zero),
                            jnp.where(lane >= HEAD_DIM, q, zero)], axis=0)


def _unstack_heads(a):
    t = a.shape[0] // 2
    lane = lax.broadcasted_iota(jnp.int32, (t, LANES), 1)
    return jnp.where(lane < HEAD_DIM, a[:t], a[t:])


def _gated(o, g_ref, rows):
    g = g_ref[0, rows, :].astype(F32)
    return (o * (g * jax.nn.sigmoid(g))).astype(BF16)


def _proj_kernel(x_ref, w_ref, wkf_ref, qscale_ref, qvg_ref, kt_ref, ft_ref):
    xb = x_ref[...].astype(BF16)
    for n in range(0, w_ref.shape[1], PROJ_TN):
        acc = _dot(xb, w_ref[:, n:n + PROJ_TN])
        if n < D_MIX:
            acc = acc * qscale_ref[:, n:n + PROJ_TN]
        qvg_ref[:, n:n + PROJ_TN] = acc.astype(BF16)
    kf = _dot_nt(wkf_ref[...], xb)
    ret0, ret1 = N_FOX_HEADS * HEAD_DIM, (N_FOX_HEADS + N_RET_HEADS) * HEAD_DIM
    kt_ref[:ret0] = kf[:ret0].astype(BF16)
    kt_ref[ret0:ret1] = (kf[ret0:ret1] * QK_SCALE).astype(BF16)
    kt_ref[ret1:] = kf[ret1:D_MIX].astype(BF16)
    ft_ref[...] = kf[D_MIX:]


def _proj(x2d, w_qvg, wkf_t, qscale):
    m, d = x2d.shape
    n = w_qvg.shape[1]
    return pl.pallas_call(
        _proj_kernel,
        grid=(m // PROJ_TM,),
        in_specs=[
            pl.BlockSpec((PROJ_TM, d), lambda i: (i, 0)),
            pl.BlockSpec((d, n), lambda i: (0, 0)),
            pl.BlockSpec((D_MIX + F_ROWS, d), lambda i: (0, 0)),
            pl.BlockSpec((1, D_MIX), lambda i: (0, 0)),
        ],
        out_specs=[
            pl.BlockSpec((PROJ_TM, n), lambda i: (i, 0)),
            pl.BlockSpec((D_MIX, PROJ_TM), lambda i: (0, i)),
            pl.BlockSpec((F_ROWS, PROJ_TM), lambda i: (0, i)),
        ],
        out_shape=[
            jax.ShapeDtypeStruct((m, n), BF16),
            jax.ShapeDtypeStruct((D_MIX, m), BF16),
            jax.ShapeDtypeStruct((F_ROWS, m), F32),
        ],
        compiler_params=pltpu.CompilerParams(
            dimension_semantics=("parallel",), vmem_limit_bytes=VMEM_LIMIT),
        name="proj",
    )(x2d, w_qvg, wkf_t, qscale)


def _fox_body(pair, q_ref, kt_ref, v_ref, g_ref, ft_ref, bf_ref, y_ref, c_scr, vaug_scr):
    s_len = q_ref.shape[1]
    nt = s_len // TILE

    f_all = ft_ref[...] + bf_ref[...]
    f_row = lax.broadcasted_iota(jnp.int32, f_all.shape, 0)
    f_sel = jnp.zeros_like(f_all)
    for h in range(PAIR):
        picked = jnp.sum(jnp.where(f_row == PAIR * pair + h, f_all, 0.0), axis=0, keepdims=True)
        f_sel = jnp.where(f_row == h, picked, f_sel)
    lsg = _log_sigmoid(f_sel)
    r = lax.broadcasted_iota(jnp.int32, (TILE, TILE), 0)
    c = lax.broadcasted_iota(jnp.int32, (TILE, TILE), 1)
    tri = jnp.where(r <= c, 1.0, 0.0).astype(BF16)
    tri3 = jnp.concatenate([tri, tri, tri], axis=0)
    tiles = jnp.concatenate([lsg[:, jb * TILE:(jb + 1) * TILE] for jb in range(nt)], axis=0)
    local = _dot(jnp.concatenate(_split_bf16(tiles, 3), axis=1), tri3)
    carry = jnp.zeros((F_ROWS, 1), F32)
    for jb in range(nt):
        blk = local[jb * F_ROWS:(jb + 1) * F_ROWS]
        c_scr[:, jb * TILE:(jb + 1) * TILE] = (blk + carry) * LOG2E
        carry = carry + blk[:, TILE - 1:TILE]

    vaug_scr[:, :LANES] = v_ref[0]
    vaug_scr[:, LANES:] = jnp.ones((s_len, LANES), BF16)

    causal = c <= r
    for qi in reversed(range(nt)):
        n = (qi + 1) * TILE
        rows = slice(qi * TILE, n)
        qq = _stack_heads(q_ref[0, rows, :])
        s = _dot(qq, kt_ref[:, :n])
        ps = []
        for h in range(PAIR):
            sh = s[h * TILE:(h + 1) * TILE] - c_scr[h:h + 1, :n]
            diag = jnp.where(causal, sh[:, n - TILE:], -jnp.inf)
            sh = diag if qi == 0 else jnp.concatenate([sh[:, :n - TILE], diag], axis=1)
            m = jnp.max(sh, axis=-1, keepdims=True)
            ps.append(jnp.exp2(sh - m).astype(BF16))
        res = _dot(jnp.concatenate(ps, axis=0), vaug_scr[:n, :])
        y_ref[0, rows, :] = _gated(_unstack_heads(res[:, :LANES] / res[:, LANES:]), g_ref, rows)


def _sb_body(q_ref, kt_ref, v_ref, g_ref, y_ref):
    s_len = q_ref.shape[1]
    nt = s_len // TILE

    r = lax.broadcasted_iota(jnp.int32, (TILE, TILE), 0)
    c = lax.broadcasted_iota(jnp.int32, (TILE, TILE), 1)
    from_key = jnp.where(r >= c, 1.0, 0.0).astype(BF16)
    from_key2 = jnp.concatenate([from_key, from_key], axis=0)
    strict1 = c < r
    strict = jnp.concatenate([strict1, strict1], axis=0)

    for qi in reversed(range(nt)):
        n = (qi + 1) * TILE
        rows = slice(qi * TILE, n)
        qq = _stack_heads(q_ref[0, rows, :])
        z2 = _dot(qq, kt_ref[:, :n])
        lsn = _log2_one_minus_sigmoid(z2)
        tail = jnp.zeros((PAIR * TILE, 1), F32)
        weights = [None] * (qi + 1)
        for j in range(qi, -1, -1):
            lsn_j = lsn[:, j * TILE:(j + 1) * TILE]
            if j == qi:
                lsn_j = jnp.where(strict, lsn_j, 0.0)
            hi, lo = _split_bf16(lsn_j, 2)
            cum = _dot(jnp.concatenate([hi, lo], axis=1), from_key2)
            a = jnp.exp2(z2[:, j * TILE:(j + 1) * TILE] + cum + tail)
            if j == qi:
                a = jnp.where(strict, a, 0.0)
            weights[j] = a.astype(BF16)
            tail = tail + cum[:, 0:1]
        acc = _dot(jnp.concatenate(weights, axis=1), v_ref[0, :n, :])
        y_ref[0, rows, :] = _gated(_unstack_heads(acc), g_ref, rows)


def _rotary(t, cos, sin_signed):
    lane = lax.broadcasted_iota(jnp.int32, t.shape, 1)
    half = HEAD_DIM // 2
    partner = jnp.where((lane & (HEAD_DIM - 1)) < half,
                        pltpu.roll(t, LANES - half, 1),
                        pltpu.roll(t, half, 1))
    return t * cos + partner * sin_signed


def _rotary_t(t, cos_t, sin_signed_t):
    half = HEAD_DIM // 2
    blocks = [t[i * half:(i + 1) * half] for i in range(LANES // half)]
    partner = jnp.concatenate([blocks[i ^ 1] for i in range(len(blocks))], axis=0)
    return t * cos_t + partner * sin_signed_t


def _ret_body(pair, lg_ref, q_ref, kt_ref, v_ref, g_ref, cos_ref, sin_ref, cost_ref, sint_ref,
              gain_ref, y_ref):
    s_len = q_ref.shape[1]
    nt = s_len // TILE
    lg0 = lg_ref[PAIR * pair]
    lg1 = lg_ref[PAIR * pair + 1]

    lane_row = lax.broadcasted_iota(jnp.int32, (1, LANES), 1)
    lg_lane = jnp.where(lane_row < HEAD_DIM, lg0, lg1)
    idx = lax.broadcasted_iota(jnp.int32, (TILE, LANES), 0).astype(F32)
    query_decay = jnp.exp((idx + 1.0) * lg_lane)
    chunk_decay = jnp.exp(float(TILE) * lg_lane)
    sub_row = lax.broadcasted_iota(jnp.int32, (LANES, TILE), 0)
    pos_t = lax.broadcasted_iota(jnp.int32, (LANES, TILE), 1).astype(F32)
    key_decay_t = jnp.exp((TILE - 1.0 - pos_t) * jnp.where(sub_row < HEAD_DIM, lg0, lg1))

    r = lax.broadcasted_iota(jnp.int32, (TILE, TILE), 0)
    c = lax.broadcasted_iota(jnp.int32, (TILE, TILE), 1)
    diff = jnp.maximum(r - c, 0).astype(F32)
    intra_decay = jnp.concatenate(
        [jnp.where(r >= c, jnp.exp(diff * lg), 0.0) for lg in (lg0, lg1)], axis=0)

    rr = lax.broadcasted_iota(jnp.int32, (LANES, LANES), 0)
    cc = lax.broadcasted_iota(jnp.int32, (LANES, LANES), 1)
    same_head = (rr < HEAD_DIM) == (cc < HEAD_DIM)
    lane = lax.broadcasted_iota(jnp.int32, (TILE, LANES), 1)
    first_head = lane < HEAD_DIM

    def head_mean(t):
        s0 = jnp.sum(jnp.where(first_head, t, 0.0), axis=-1, keepdims=True)
        s_all = jnp.sum(t, axis=-1, keepdims=True)
        return jnp.where(first_head, s0, s_all - s0) * (1.0 / HEAD_DIM)

    state = jnp.zeros((LANES, LANES), F32)
    for i in range(nt):
        rows = slice(i * TILE, (i + 1) * TILE)
        qc = _rotary(q_ref[0, rows, :].astype(F32), cos_ref[rows, :], sin_ref[rows, :]).astype(BF16)
        kt = _rotary_t(kt_ref[:, rows].astype(F32), cost_ref[:, rows], sint_ref[:, rows])
        vc = v_ref[0, rows, :]
        scores = _dot(_stack_heads(qc), kt.astype(BF16)) * intra_decay
        o = _unstack_heads(_dot(scores.astype(BF16), vc))
        o = o + _dot(qc, state.astype(BF16)) * query_decay
        kd = (kt * key_decay_t).astype(BF16)
        state = chunk_decay * state + jnp.where(same_head, _dot(kd, vc), 0.0)

        d = o - head_mean(o)
        var = head_mean(d * d)
        y_ref[0, rows, :] = _gated(d * lax.rsqrt(var + GN_EPS) * gain_ref[...], g_ref, rows)


def _mixer_kernel(lg_ref, q_ref, kt_ref, v_ref, g_ref, ft_ref, bf_ref, cos_ref, sin_ref, cost_ref,
                  sint_ref, gain_ref, y_ref, c_scr, vaug_scr):
    pair = pl.program_id(1)
    ret0 = N_FOX_PAIRS
    sb0 = N_FOX_PAIRS + N_RET_PAIRS

    @pl.when(pair < ret0)
    def _():
        _fox_body(pair, q_ref, kt_ref, v_ref, g_ref, ft_ref, bf_ref, y_ref, c_scr, vaug_scr)

    @pl.when(jnp.logical_and(pair >= ret0, pair < sb0))
    def _():
        _ret_body(pair - ret0, lg_ref, q_ref, kt_ref, v_ref, g_ref, cos_ref, sin_ref, cost_ref,
                  sint_ref, gain_ref, y_ref)

    @pl.when(pair >= sb0)
    def _():
        _sb_body(q_ref, kt_ref, v_ref, g_ref, y_ref)


def _mixer(qvg, kt, f_t, b_f, log_g, tables, gain, batch, s_len):
    blk = (1, s_len, LANES)
    tab = pl.BlockSpec((s_len, LANES), lambda b, p: (0, 0))
    tab_t = pl.BlockSpec((LANES, s_len), lambda b, p: (0, 0))
    return pl.pallas_call(
        _mixer_kernel,
        grid=(batch, N_PAIRS),
        in_specs=[
            pl.BlockSpec(memory_space=pltpu.SMEM),
            pl.BlockSpec(blk, lambda b, p: (b, 0, p)),
            pl.BlockSpec((LANES, s_len), lambda b, p: (p, b)),
            pl.BlockSpec(blk, lambda b, p: (b, 0, N_PAIRS + p)),
            pl.BlockSpec(blk, lambda b, p: (b, 0, 2 * N_PAIRS + p)),
            pl.BlockSpec((F_ROWS, s_len), lambda b, p: (0, b)),
            pl.BlockSpec((F_ROWS, 1), lambda b, p: (0, 0)),
            tab, tab, tab_t, tab_t,
            pl.BlockSpec((1, LANES),
                         lambda b, p: (0, jnp.clip(p - N_FOX_PAIRS, 0, N_RET_PAIRS - 1))),
        ],
        out_specs=pl.BlockSpec(blk, lambda b, p: (b, 0, p)),
        out_shape=jax.ShapeDtypeStruct((batch, s_len, D_MIX), BF16),
        scratch_shapes=[pltpu.VMEM((F_ROWS, s_len), F32),
                        pltpu.VMEM((s_len, 2 * LANES), BF16)],
        compiler_params=pltpu.CompilerParams(
            dimension_semantics=("parallel", "parallel"), vmem_limit_bytes=VMEM_LIMIT),
        name="mixer",
    )(log_g, qvg, kt, qvg, qvg, f_t, b_f, *tables, gain)


def _out_kernel(y_ref, x_ref, w_ref, gain_ref, bias_ref, o_ref, *, alpha):
    h = alpha * x_ref[...] + _dot(y_ref[...], w_ref[...])
    mu = jnp.mean(h, axis=-1, keepdims=True)
    d = h - mu
    var = jnp.mean(d * d, axis=-1, keepdims=True)
    o_ref[...] = d * lax.rsqrt(var + LN_EPS) * gain_ref[...] + bias_ref[...]


def _out(y2d, x2d, w_out, gain, bias, alpha):
    m, d = x2d.shape
    row = lambda n: pl.BlockSpec((PROJ_TM, n), lambda i: (i, 0))
    vec = pl.BlockSpec((1, d), lambda i: (0, 0))
    return pl.pallas_call(
        functools.partial(_out_kernel, alpha=alpha),
        grid=(m // PROJ_TM,),
        in_specs=[row(D_MIX), row(d), pl.BlockSpec((D_MIX, d), lambda i: (0, 0)), vec, vec],
        out_specs=row(d),
        out_shape=jax.ShapeDtypeStruct((m, d), F32),
        compiler_params=pltpu.CompilerParams(
            dimension_semantics=("parallel",), vmem_limit_bytes=VMEM_LIMIT),
        name="out_proj_ln",
    )(y2d, x2d, w_out, gain, bias)


def _rope_tables(s_len):
    half = HEAD_DIM // 2
    inv_freq = 1.0 / (ROPE_BASE ** (jnp.arange(half, dtype=F32) / half))
    ang = jnp.arange(s_len, dtype=F32)[:, None] * inv_freq[None, :]
    cos, sin = jnp.cos(ang), jnp.sin(ang)
    cos_p = jnp.tile(jnp.concatenate([cos, cos], axis=-1), (1, PAIR))
    sin_p = jnp.tile(jnp.concatenate([-sin, sin], axis=-1), (1, PAIR))
    return cos_p, sin_p, cos_p.T, sin_p.T


def kernel(x, w_in, b_fgate, ret_gn_gain, w_out, ln_gain, ln_bias):
    batch, s_len, d_model = x.shape
    depth = w_in.shape[0]
    alpha = (2 * depth) ** 0.25
    m = batch * s_len

    tables = _rope_tables(s_len)
    log_g = jnp.log(1.0 - 2.0 ** (-5.0 - jnp.arange(N_RET_HEADS, dtype=F32)))
    head_scale = ([QK_SCALE * LOG2E] * N_FOX_HEADS + [1.0] * N_RET_HEADS
                  + [QK_SCALE * LOG2E] * N_SB_HEADS)
    qscale = jnp.repeat(jnp.asarray(head_scale, F32), HEAD_DIM).reshape(1, D_MIX)

    x2d = x.reshape(m, d_model)
    for layer in range(depth):
        w = w_in[layer]
        w_qvg = jnp.concatenate([w[:, :D_MIX], w[:, 2 * D_MIX:4 * D_MIX]], axis=1).astype(BF16)
        wkf_t = jnp.zeros((D_MIX + F_ROWS, d_model), BF16)
        wkf_t = wkf_t.at[:D_MIX].set(w[:, D_MIX:2 * D_MIX].T.astype(BF16))
        wkf_t = wkf_t.at[D_MIX:D_MIX + N_FOX_HEADS].set(w[:, 4 * D_MIX:].T.astype(BF16))
        b_f = jnp.zeros((F_ROWS, 1), F32).at[:N_FOX_HEADS, 0].set(b_fgate[layer])

        qvg2d, kt, f_t = _proj(x2d, w_qvg, wkf_t, qscale)
        qvg = qvg2d.reshape(batch, s_len, 3 * D_MIX)
        y = _mixer(qvg, kt, f_t, b_f, log_g, tables, ret_gn_gain[layer].reshape(1, -1),
                   batch, s_len)
        x2d = _out(y.reshape(m, D_MIX), x2d, w_out[layer].astype(BF16),
                   ln_gain[layer].reshape(1, -1), ln_bias[layer].reshape(1, -1), alpha)
    return x2d.reshape(batch, s_len, d_model)
```

```python
import functools
import math

import jax
import jax.numpy as jnp
from jax import lax
from jax.experimental import pallas as pl
from jax.experimental.pallas import tpu as pltpu

F32 = jnp.float32
BF16 = jnp.bfloat16

HEAD_DIM = 64
N_FOX_HEADS = 6
N_RET_HEADS = 6
N_SB_HEADS = 4
N_HEADS = N_FOX_HEADS + N_RET_HEADS + N_SB_HEADS
D_MIX = N_HEADS * HEAD_DIM
ROPE_BASE = 10000.0
LN_EPS = 1e-5
GN_EPS = 1e-5
QK_SCALE = HEAD_DIM ** -0.5
LOG2E = math.log2(math.e)

LANES = 128
PAIR = LANES // HEAD_DIM
N_FOX_PAIRS = N_FOX_HEADS // PAIR
N_RET_PAIRS = N_RET_HEADS // PAIR
N_SB_PAIRS = N_SB_HEADS // PAIR
N_PAIRS = N_HEADS // PAIR
F_ROWS = 8

TILE = 256
PROJ_TM = 512
PROJ_TN = 512
OUT_TM = 1024
VMEM_LIMIT = 48 * 1024 * 1024


def _dot(a, b):
    return jnp.dot(a, b, preferred_element_type=F32)


def _dot_nt(a, b):
    return lax.dot_general(a, b, (((1,), (1,)), ((), ())), preferred_element_type=F32)


def _split_bf16(x, parts):
    out = []
    for _ in range(parts - 1):
        p = x.astype(BF16)
        out.append(p)
        x = x - p.astype(F32)
    out.append(x.astype(BF16))
    return out


def _log_sigmoid(z):
    return jnp.minimum(z, 0.0) - jnp.log1p(jnp.exp(-jnp.abs(z)))


def _log2_one_minus_sigmoid(z2):
    nz = -z2
    return jnp.minimum(nz, 0.0) - jnp.log2(1.0 + jnp.exp2(jnp.minimum(z2, nz)))


def _stack_heads(q):
    lane = lax.broadcasted_iota(jnp.int32, q.shape, 1)
    zero = jnp.zeros_like(q)
    return jnp.concatenate([jnp.where(lane < HEAD_DIM, q, zero),
                            jnp.where(lane >= HEAD_DIM, q, zero)], axis=0)


def _unstack_heads(a):
    t = a.shape[0] // 2
    lane = lax.broadcasted_iota(jnp.int32, (t, LANES), 1)
    return jnp.where(lane < HEAD_DIM, a[:t], a[t:])


def _gated(o, g_ref, rows):
    g = g_ref[0, rows, :].astype(F32)
    return (o * (g * jax.nn.sigmoid(g))).astype(BF16)


def _project(xb, w_ref, wkf_ref, qscale_ref, qvg_ref, kt_ref, ft_ref):
    for n in range(0, w_ref.shape[1], PROJ_TN):
        acc = _dot(xb, w_ref[:, n:n + PROJ_TN])
        if n < D_MIX:
            acc = acc * qscale_ref[:, n:n + PROJ_TN]
        qvg_ref[:, n:n + PROJ_TN] = acc.astype(BF16)
    kf = _dot_nt(wkf_ref[...], xb)
    ret0, ret1 = N_FOX_HEADS * HEAD_DIM, (N_FOX_HEADS + N_RET_HEADS) * HEAD_DIM
    kt_ref[:ret0] = kf[:ret0].astype(BF16)
    kt_ref[ret0:ret1] = (kf[ret0:ret1] * QK_SCALE).astype(BF16)
    kt_ref[ret1:] = kf[ret1:D_MIX].astype(BF16)
    ft_ref[...] = kf[D_MIX:]


def _proj_kernel(x_ref, w_ref, wkf_ref, qscale_ref, qvg_ref, kt_ref, ft_ref):
    _project(x_ref[...].astype(BF16), w_ref, wkf_ref, qscale_ref, qvg_ref, kt_ref, ft_ref)


def _resident(shape):
    return pl.BlockSpec(shape, lambda i: (0, 0), pipeline_mode=pl.Buffered(1))


def _proj_specs(m, d, n):
    in_specs = [_resident((d, n)), _resident((D_MIX + F_ROWS, d)), _resident((1, D_MIX))]
    out_specs = [
        pl.BlockSpec((PROJ_TM, n), lambda i: (i, 0)),
        pl.BlockSpec((D_MIX, PROJ_TM), lambda i: (0, i)),
        pl.BlockSpec((F_ROWS, PROJ_TM), lambda i: (0, i)),
    ]
    out_shape = [
        jax.ShapeDtypeStruct((m, n), BF16),
        jax.ShapeDtypeStruct((D_MIX, m), BF16),
        jax.ShapeDtypeStruct((F_ROWS, m), F32),
    ]
    return in_specs, out_specs, out_shape


def _proj(x2d, w_qvg, wkf_t, qscale):
    m, d = x2d.shape
    in_specs, out_specs, out_shape = _proj_specs(m, d, w_qvg.shape[1])
    return pl.pallas_call(
        _proj_kernel,
        grid=(m // PROJ_TM,),
        in_specs=[pl.BlockSpec((PROJ_TM, d), lambda i: (i, 0))] + in_specs,
        out_specs=out_specs,
        out_shape=out_shape,
        compiler_params=pltpu.CompilerParams(
            dimension_semantics=("parallel",), vmem_limit_bytes=VMEM_LIMIT),
        name="proj",
    )(x2d, w_qvg, wkf_t, qscale)


def _fox_body(pair, q_ref, kt_ref, v_ref, g_ref, ft_ref, bf_ref, y_ref, c_scr, vaug_scr):
    s_len = q_ref.shape[1]
    nt = s_len // TILE

    f_all = ft_ref[...] + bf_ref[...]
    f_row = lax.broadcasted_iota(jnp.int32, f_all.shape, 0)
    f_sel = jnp.zeros_like(f_all)
    for h in range(PAIR):
        picked = jnp.sum(jnp.where(f_row == PAIR * pair + h, f_all, 0.0), axis=0, keepdims=True)
        f_sel = jnp.where(f_row == h, picked, f_sel)
    lsg = _log_sigmoid(f_sel)
    r = lax.broadcasted_iota(jnp.int32, (TILE, TILE), 0)
    c = lax.broadcasted_iota(jnp.int32, (TILE, TILE), 1)
    tri = jnp.where(r <= c, 1.0, 0.0).astype(BF16)
    tri3 = jnp.concatenate([tri, tri, tri], axis=0)
    tiles = jnp.concatenate([lsg[:, jb * TILE:(jb + 1) * TILE] for jb in range(nt)], axis=0)
    local = _dot(jnp.concatenate(_split_bf16(tiles, 3), axis=1), tri3)
    carry = jnp.zeros((F_ROWS, 1), F32)
    for jb in range(nt):
        blk = local[jb * F_ROWS:(jb + 1) * F_ROWS]
        c_scr[:, jb * TILE:(jb + 1) * TILE] = (blk + carry) * LOG2E
        carry = carry + blk[:, TILE - 1:TILE]

    vaug_scr[:, :LANES] = v_ref[0]
    vaug_scr[:, LANES:] = jnp.ones((s_len, LANES), BF16)

    causal = c <= r
    for qi in reversed(range(nt)):
        n = (qi + 1) * TILE
        rows = slice(qi * TILE, n)
        qq = _stack_heads(q_ref[0, rows, :])
        s = _dot(qq, kt_ref[:, :n])
        ps = []
        for h in range(PAIR):
            sh = s[h * TILE:(h + 1) * TILE] - c_scr[h:h + 1, :n]
            diag = jnp.where(causal, sh[:, n - TILE:], -jnp.inf)
            sh = diag if qi == 0 else jnp.concatenate([sh[:, :n - TILE], diag], axis=1)
            m = jnp.max(sh, axis=-1, keepdims=True)
            ps.append(jnp.exp2(sh - m).astype(BF16))
        res = _dot(jnp.concatenate(ps, axis=0), vaug_scr[:n, :])
        y_ref[0, rows, :] = _gated(_unstack_heads(res[:, :LANES] / res[:, LANES:]), g_ref, rows)


def _sb_body(q_ref, kt_ref, v_ref, g_ref, y_ref):
    s_len = q_ref.shape[1]
    nt = s_len // TILE

    r = lax.broadcasted_iota(jnp.int32, (TILE, TILE), 0)
    c = lax.broadcasted_iota(jnp.int32, (TILE, TILE), 1)
    from_key = jnp.where(r >= c, 1.0, 0.0).astype(BF16)
    from_key2 = jnp.concatenate([from_key, from_key], axis=0)
    strict1 = c < r
    strict = jnp.concatenate([strict1, strict1], axis=0)

    for qi in reversed(range(nt)):
        n = (qi + 1) * TILE
        rows = slice(qi * TILE, n)
        qq = _stack_heads(q_ref[0, rows, :])
        z2 = _dot(qq, kt_ref[:, :n])
        lsn = _log2_one_minus_sigmoid(z2)
        tail = jnp.zeros((PAIR * TILE, 1), F32)
        weights = [None] * (qi + 1)
        for j in range(qi, -1, -1):
            lsn_j = lsn[:, j * TILE:(j + 1) * TILE]
            if j == qi:
                lsn_j = jnp.where(strict, lsn_j, 0.0)
            hi, lo = _split_bf16(lsn_j, 2)
            cum = _dot(jnp.concatenate([hi, lo], axis=1), from_key2)
            a = jnp.exp2(z2[:, j * TILE:(j + 1) * TILE] + cum + tail)
            if j == qi:
                a = jnp.where(strict, a, 0.0)
            weights[j] = a.astype(BF16)
            tail = tail + cum[:, 0:1]
        acc = _dot(jnp.concatenate(weights, axis=1), v_ref[0, :n, :])
        y_ref[0, rows, :] = _gated(_unstack_heads(acc), g_ref, rows)


def _rotary(t, cos, sin_signed):
    lane = lax.broadcasted_iota(jnp.int32, t.shape, 1)
    half = HEAD_DIM // 2
    partner = jnp.where((lane & (HEAD_DIM - 1)) < half,
                        pltpu.roll(t, LANES - half, 1),
                        pltpu.roll(t, half, 1))
    return t * cos + partner * sin_signed


def _rotary_t(t, cos_t, sin_signed_t):
    half = HEAD_DIM // 2
    blocks = [t[i * half:(i + 1) * half] for i in range(LANES // half)]
    partner = jnp.concatenate([blocks[i ^ 1] for i in range(len(blocks))], axis=0)
    return t * cos_t + partner * sin_signed_t


def _ret_body(pair, lg_ref, q_ref, kt_ref, v_ref, g_ref, cos_ref, sin_ref, cost_ref, sint_ref,
              gain_ref, y_ref):
    s_len = q_ref.shape[1]
    nt = s_len // TILE
    lg0 = lg_ref[PAIR * pair]
    lg1 = lg_ref[PAIR * pair + 1]

    lane_row = lax.broadcasted_iota(jnp.int32, (1, LANES), 1)
    lg_lane = jnp.where(lane_row < HEAD_DIM, lg0, lg1)
    idx = lax.broadcasted_iota(jnp.int32, (TILE, LANES), 0).astype(F32)
    query_decay = jnp.exp((idx + 1.0) * lg_lane)
    chunk_decay = jnp.exp(float(TILE) * lg_lane)
    sub_row = lax.broadcasted_iota(jnp.int32, (LANES, TILE), 0)
    pos_t = lax.broadcasted_iota(jnp.int32, (LANES, TILE), 1).astype(F32)
    key_decay_t = jnp.exp((TILE - 1.0 - pos_t) * jnp.where(sub_row < HEAD_DIM, lg0, lg1))

    r = lax.broadcasted_iota(jnp.int32, (TILE, TILE), 0)
    c = lax.broadcasted_iota(jnp.int32, (TILE, TILE), 1)
    diff = jnp.maximum(r - c, 0).astype(F32)
    intra_decay = jnp.concatenate(
        [jnp.where(r >= c, jnp.exp(diff * lg), 0.0) for lg in (lg0, lg1)], axis=0)

    rr = lax.broadcasted_iota(jnp.int32, (LANES, LANES), 0)
    cc = lax.broadcasted_iota(jnp.int32, (LANES, LANES), 1)
    same_head = (rr < HEAD_DIM) == (cc < HEAD_DIM)
    lane = lax.broadcasted_iota(jnp.int32, (TILE, LANES), 1)
    first_head = lane < HEAD_DIM

    def head_mean(t):
        s0 = jnp.sum(jnp.where(first_head, t, 0.0), axis=-1, keepdims=True)
        s_all = jnp.sum(t, axis=-1, keepdims=True)
        return jnp.where(first_head, s0, s_all - s0) * (1.0 / HEAD_DIM)

    state = jnp.zeros((LANES, LANES), F32)
    for i in range(nt):
        rows = slice(i * TILE, (i + 1) * TILE)
        qc = _rotary(q_ref[0, rows, :].astype(F32), cos_ref[rows, :], sin_ref[rows, :]).astype(BF16)
        kt = _rotary_t(kt_ref[:, rows].astype(F32), cost_ref[:, rows], sint_ref[:, rows])
        vc = v_ref[0, rows, :]
        scores = _dot(_stack_heads(qc), kt.astype(BF16)) * intra_decay
        o = _unstack_heads(_dot(scores.astype(BF16), vc))
        o = o + _dot(qc, state.astype(BF16)) * query_decay
        kd = (kt * key_decay_t).astype(BF16)
        state = chunk_decay * state + jnp.where(same_head, _dot(kd, vc), 0.0)

        d = o - head_mean(o)
        var = head_mean(d * d)
        y_ref[0, rows, :] = _gated(d * lax.rsqrt(var + GN_EPS) * gain_ref[...], g_ref, rows)


def _mixer_kernel(lg_ref, q_ref, kt_ref, v_ref, g_ref, ft_ref, bf_ref, cos_ref, sin_ref, cost_ref,
                  sint_ref, gain_ref, y_ref, c_scr, vaug_scr):
    pair = pl.program_id(1)
    ret0 = N_FOX_PAIRS
    sb0 = N_FOX_PAIRS + N_RET_PAIRS

    @pl.when(pair < ret0)
    def _():
        _fox_body(pair, q_ref, kt_ref, v_ref, g_ref, ft_ref, bf_ref, y_ref, c_scr, vaug_scr)

    @pl.when(jnp.logical_and(pair >= ret0, pair < sb0))
    def _():
        _ret_body(pair - ret0, lg_ref, q_ref, kt_ref, v_ref, g_ref, cos_ref, sin_ref, cost_ref,
                  sint_ref, gain_ref, y_ref)

    @pl.when(pair >= sb0)
    def _():
        _sb_body(q_ref, kt_ref, v_ref, g_ref, y_ref)


def _mixer(qvg, kt, f_t, b_f, log_g, tables, gain, batch, s_len):
    blk = (1, s_len, LANES)
    tab = pl.BlockSpec((s_len, LANES), lambda b, p: (0, 0))
    tab_t = pl.BlockSpec((LANES, s_len), lambda b, p: (0, 0))
    return pl.pallas_call(
        _mixer_kernel,
        grid=(batch, N_PAIRS),
        in_specs=[
            pl.BlockSpec(memory_space=pltpu.SMEM),
            pl.BlockSpec(blk, lambda b, p: (b, 0, p)),
            pl.BlockSpec((LANES, s_len), lambda b, p: (p, b)),
            pl.BlockSpec(blk, lambda b, p: (b, 0, N_PAIRS + p)),
            pl.BlockSpec(blk, lambda b, p: (b, 0, 2 * N_PAIRS + p)),
            pl.BlockSpec((F_ROWS, s_len), lambda b, p: (0, b)),
            pl.BlockSpec((F_ROWS, 1), lambda b, p: (0, 0)),
            tab, tab, tab_t, tab_t,
            pl.BlockSpec((1, LANES),
                         lambda b, p: (0, jnp.clip(p - N_FOX_PAIRS, 0, N_RET_PAIRS - 1))),
        ],
        out_specs=pl.BlockSpec(blk, lambda b, p: (b, 0, p)),
        out_shape=jax.ShapeDtypeStruct((batch, s_len, D_MIX), BF16),
        scratch_shapes=[pltpu.VMEM((F_ROWS, s_len), F32),
                        pltpu.VMEM((s_len, 2 * LANES), BF16)],
        compiler_params=pltpu.CompilerParams(
            dimension_semantics=("parallel", "parallel"), vmem_limit_bytes=VMEM_LIMIT),
        name="mixer",
    )(log_g, qvg, kt, qvg, qvg, f_t, b_f, *tables, gain)


def _out_kernel(y_ref, x_ref, w_ref, gain_ref, bias_ref, *rest, alpha):
    o_ref = rest[0] if len(rest) == 1 else rest[3]
    h = alpha * x_ref[...] + _dot(y_ref[...], w_ref[...])
    mu = jnp.mean(h, axis=-1, keepdims=True)
    d = h - mu
    var = jnp.mean(d * d, axis=-1, keepdims=True)
    x_next = d * lax.rsqrt(var + LN_EPS) * gain_ref[...] + bias_ref[...]
    o_ref[...] = x_next
    if len(rest) > 1:
        _project(x_next.astype(BF16), *rest[:3], *rest[4:])


def _out(y2d, x2d, w_out, gain, bias, alpha, next_proj=None):
    m, d = x2d.shape
    tm = OUT_TM if next_proj is None else PROJ_TM
    assert m % tm == 0, (m, tm)
    row = lambda n: pl.BlockSpec((tm, n), lambda i: (i, 0))
    in_specs = [row(D_MIX), row(d), _resident((D_MIX, d)), _resident((1, d)), _resident((1, d))]
    out_specs = [row(d)]
    out_shape = [jax.ShapeDtypeStruct((m, d), F32)]
    operands = [y2d, x2d, w_out, gain, bias]
    if next_proj is not None:
        p_in, p_out, p_shape = _proj_specs(m, d, next_proj[0].shape[1])
        in_specs += p_in
        out_specs += p_out
        out_shape += p_shape
        operands += list(next_proj)
    return pl.pallas_call(
        functools.partial(_out_kernel, alpha=alpha),
        grid=(m // tm,),
        in_specs=in_specs,
        out_specs=out_specs,
        out_shape=out_shape,
        compiler_params=pltpu.CompilerParams(
            dimension_semantics=("parallel",), vmem_limit_bytes=VMEM_LIMIT),
        name="out_proj_ln" if next_proj is None else "out_proj_ln_proj",
    )(*operands)


def _rope_tables(s_len):
    half = HEAD_DIM // 2
    inv_freq = 1.0 / (ROPE_BASE ** (jnp.arange(half, dtype=F32) / half))
    ang = jnp.arange(s_len, dtype=F32)[:, None] * inv_freq[None, :]
    cos, sin = jnp.cos(ang), jnp.sin(ang)
    cos_p = jnp.tile(jnp.concatenate([cos, cos], axis=-1), (1, PAIR))
    sin_p = jnp.tile(jnp.concatenate([-sin, sin], axis=-1), (1, PAIR))
    return cos_p, sin_p, cos_p.T, sin_p.T


def kernel(x, w_in, b_fgate, ret_gn_gain, w_out, ln_gain, ln_bias):
    batch, s_len, d_model = x.shape
    depth = w_in.shape[0]
    alpha = (2 * depth) ** 0.25
    m = batch * s_len

    tables = _rope_tables(s_len)
    log_g = jnp.log(1.0 - 2.0 ** (-5.0 - jnp.arange(N_RET_HEADS, dtype=F32)))
    head_scale = ([QK_SCALE * LOG2E] * N_FOX_HEADS + [1.0] * N_RET_HEADS
                  + [QK_SCALE * LOG2E] * N_SB_HEADS)
    qscale = jnp.repeat(jnp.asarray(head_scale, F32), HEAD_DIM).reshape(1, D_MIX)

    def proj_weights(layer):
        w = w_in[layer]
        w_qvg = jnp.concatenate([w[:, :D_MIX], w[:, 2 * D_MIX:4 * D_MIX]], axis=1).astype(BF16)
        wkf_t = jnp.zeros((D_MIX + F_ROWS, d_model), BF16)
        wkf_t = wkf_t.at[:D_MIX].set(w[:, D_MIX:2 * D_MIX].T.astype(BF16))
        wkf_t = wkf_t.at[D_MIX:D_MIX + N_FOX_HEADS].set(w[:, 4 * D_MIX:].T.astype(BF16))
        return w_qvg, wkf_t, qscale

    x2d = x.reshape(m, d_model)
    qvg2d, kt, f_t = _proj(x2d, *proj_weights(0))
    for layer in range(depth):
        b_f = jnp.zeros((F_ROWS, 1), F32).at[:N_FOX_HEADS, 0].set(b_fgate[layer])
        qvg = qvg2d.reshape(batch, s_len, 3 * D_MIX)
        y = _mixer(qvg, kt, f_t, b_f, log_g, tables, ret_gn_gain[layer].reshape(1, -1),
                   batch, s_len)
        next_proj = proj_weights(layer + 1) if layer + 1 < depth else None
        outs = _out(y.reshape(m, D_MIX), x2d, w_out[layer].astype(BF16),
                    ln_gain[layer].reshape(1, -1), ln_bias[layer].reshape(1, -1), alpha, next_proj)
        x2d = outs[0]
        if next_proj is not None:
            qvg2d, kt, f_t = outs[1:]
    return x2d.reshape(batch, s_len, d_model)
```

```python
import functools
import math

import jax
import jax.numpy as jnp
import numpy as np
from jax import lax
from jax.experimental import pallas as pl
from jax.experimental.pallas import tpu as pltpu

F32 = jnp.float32
BF16 = jnp.bfloat16

HEAD_DIM = 64
N_FOX_HEADS = 6
N_RET_HEADS = 6
N_SB_HEADS = 4
N_HEADS = N_FOX_HEADS + N_RET_HEADS + N_SB_HEADS
D_MIX = N_HEADS * HEAD_DIM
ROPE_BASE = 10000.0
LN_EPS = 1e-5
GN_EPS = 1e-5
QK_SCALE = HEAD_DIM ** -0.5
LOG2E = math.log2(math.e)

LANES = 128
PAIR = LANES // HEAD_DIM
N_FOX_PAIRS = N_FOX_HEADS // PAIR
N_RET_PAIRS = N_RET_HEADS // PAIR
N_SB_PAIRS = N_SB_HEADS // PAIR
N_PAIRS = N_HEADS // PAIR
F_ROWS = 8

TILE = 256
PROJ_TM = 512
PROJ_TN = 512
OUT_TM = 1024
VMEM_LIMIT = 48 * 1024 * 1024


def _dot(a, b):
    return jnp.dot(a, b, preferred_element_type=F32)


def _dot_nt(a, b):
    return lax.dot_general(a, b, (((1,), (1,)), ((), ())), preferred_element_type=F32)


def _split_bf16(x, parts):
    out = []
    for _ in range(parts - 1):
        p = x.astype(BF16)
        out.append(p)
        x = x - p.astype(F32)
    out.append(x.astype(BF16))
    return out


def _log_sigmoid(z):
    return jnp.minimum(z, 0.0) - jnp.log1p(jnp.exp(-jnp.abs(z)))


def _log2_one_minus_sigmoid(z2):
    nz = -z2
    return jnp.minimum(nz, 0.0) - jnp.log2(1.0 + jnp.exp2(jnp.minimum(z2, nz)))


def _stack_heads(q):
    lane = lax.broadcasted_iota(jnp.int32, q.shape, 1)
    zero = jnp.zeros_like(q)
    return jnp.concatenate([jnp.where(lane < HEAD_DIM, q, zero),
                            jnp.where(lane >= HEAD_DIM, q, zero)], axis=0)


def _unstack_heads(a):
    t = a.shape[0] // 2
    lane = lax.broadcasted_iota(jnp.int32, (t, LANES), 1)
    return jnp.where(lane < HEAD_DIM, a[:t], a[t:])


def _gated(o, g_ref, rows):
    g = g_ref[0, rows, :].astype(F32)
    return (o * (g * jax.nn.sigmoid(g))).astype(BF16)


def _project(xb, wt_ref, wkf_ref, qscale_ref, qvg_ref, kt_ref, ft_ref):
    for n in range(0, wt_ref.shape[0], PROJ_TN):
        acc = _dot_nt(xb, wt_ref[n:n + PROJ_TN, :])
        if n < D_MIX:
            acc = acc * qscale_ref[:, n:n + PROJ_TN]
        qvg_ref[:, n:n + PROJ_TN] = acc.astype(BF16)
    kf = _dot_nt(wkf_ref[...], xb)
    ret0, ret1 = N_FOX_HEADS * HEAD_DIM, (N_FOX_HEADS + N_RET_HEADS) * HEAD_DIM
    kt_ref[:ret0] = kf[:ret0].astype(BF16)
    kt_ref[ret0:ret1] = (kf[ret0:ret1] * QK_SCALE).astype(BF16)
    kt_ref[ret1:] = kf[ret1:D_MIX].astype(BF16)
    ft_ref[...] = kf[D_MIX:]


def _proj_kernel(x_ref, wt_ref, wkf_ref, qscale_ref, qvg_ref, kt_ref, ft_ref):
    _project(x_ref[...].astype(BF16), wt_ref, wkf_ref, qscale_ref, qvg_ref, kt_ref, ft_ref)


def _resident(shape):
    return pl.BlockSpec(shape, lambda i: (0, 0), pipeline_mode=pl.Buffered(1))


def _proj_specs(m, d, n):
    in_specs = [_resident((n, d)), _resident((D_MIX + F_ROWS, d)), _resident((1, D_MIX))]
    out_specs = [
        pl.BlockSpec((PROJ_TM, n), lambda i: (i, 0)),
        pl.BlockSpec((D_MIX, PROJ_TM), lambda i: (0, i)),
        pl.BlockSpec((F_ROWS, PROJ_TM), lambda i: (0, i)),
    ]
    out_shape = [
        jax.ShapeDtypeStruct((m, n), BF16),
        jax.ShapeDtypeStruct((D_MIX, m), BF16),
        jax.ShapeDtypeStruct((F_ROWS, m), F32),
    ]
    return in_specs, out_specs, out_shape


def _proj(x2d, wqvg_t, wkf_t, qscale):
    m, d = x2d.shape
    in_specs, out_specs, out_shape = _proj_specs(m, d, wqvg_t.shape[0])
    return pl.pallas_call(
        _proj_kernel,
        grid=(m // PROJ_TM,),
        in_specs=[pl.BlockSpec((PROJ_TM, d), lambda i: (i, 0))] + in_specs,
        out_specs=out_specs,
        out_shape=out_shape,
        compiler_params=pltpu.CompilerParams(
            dimension_semantics=("parallel",), vmem_limit_bytes=VMEM_LIMIT),
        name="proj",
    )(x2d, wqvg_t, wkf_t, qscale)


def _fox_body(pair, q_ref, kt_ref, v_ref, g_ref, ft_ref, bf_ref, y_ref, c_scr, vaug_scr):
    s_len = q_ref.shape[1]
    nt = s_len // TILE

    f_all = ft_ref[...] + bf_ref[...]
    f_row = lax.broadcasted_iota(jnp.int32, f_all.shape, 0)
    f_sel = jnp.zeros_like(f_all)
    for h in range(PAIR):
        picked = jnp.sum(jnp.where(f_row == PAIR * pair + h, f_all, 0.0), axis=0, keepdims=True)
        f_sel = jnp.where(f_row == h, picked, f_sel)
    lsg = _log_sigmoid(f_sel)
    r = lax.broadcasted_iota(jnp.int32, (TILE, TILE), 0)
    c = lax.broadcasted_iota(jnp.int32, (TILE, TILE), 1)
    tri = jnp.where(r <= c, 1.0, 0.0).astype(BF16)
    tri3 = jnp.concatenate([tri, tri, tri], axis=0)
    tiles = jnp.concatenate([lsg[:, jb * TILE:(jb + 1) * TILE] for jb in range(nt)], axis=0)
    local = _dot(jnp.concatenate(_split_bf16(tiles, 3), axis=1), tri3)
    carry = jnp.zeros((F_ROWS, 1), F32)
    for jb in range(nt):
        blk = local[jb * F_ROWS:(jb + 1) * F_ROWS]
        c_scr[:, jb * TILE:(jb + 1) * TILE] = (blk + carry) * LOG2E
        carry = carry + blk[:, TILE - 1:TILE]

    vaug_scr[:, :LANES] = v_ref[0]
    vaug_scr[:, LANES:] = jnp.ones((s_len, LANES), BF16)

    causal = c <= r
    for qi in reversed(range(nt)):
        n = (qi + 1) * TILE
        rows = slice(qi * TILE, n)
        qq = _stack_heads(q_ref[0, rows, :])
        s = _dot(qq, kt_ref[:, :n])
        ps = []
        for h in range(PAIR):
            sh = s[h * TILE:(h + 1) * TILE] - c_scr[h:h + 1, :n]
            diag = jnp.where(causal, sh[:, n - TILE:], -jnp.inf)
            sh = diag if qi == 0 else jnp.concatenate([sh[:, :n - TILE], diag], axis=1)
            m = jnp.max(sh, axis=-1, keepdims=True)
            ps.append(jnp.exp2(sh - m).astype(BF16))
        res = _dot(jnp.concatenate(ps, axis=0), vaug_scr[:n, :])
        y_ref[0, rows, :] = _gated(_unstack_heads(res[:, :LANES] / res[:, LANES:]), g_ref, rows)


def _sb_body(q_ref, kt_ref, v_ref, g_ref, y_ref):
    s_len = q_ref.shape[1]
    nt = s_len // TILE

    r = lax.broadcasted_iota(jnp.int32, (TILE, TILE), 0)
    c = lax.broadcasted_iota(jnp.int32, (TILE, TILE), 1)
    from_key = jnp.where(r >= c, 1.0, 0.0).astype(BF16)
    from_key2 = jnp.concatenate([from_key, from_key], axis=0)
    strict1 = c < r
    strict = jnp.concatenate([strict1, strict1], axis=0)

    for qi in reversed(range(nt)):
        n = (qi + 1) * TILE
        rows = slice(qi * TILE, n)
        qq = _stack_heads(q_ref[0, rows, :])
        z2 = _dot(qq, kt_ref[:, :n])
        lsn = _log2_one_minus_sigmoid(z2)
        tail = jnp.zeros((PAIR * TILE, 1), F32)
        weights = [None] * (qi + 1)
        for j in range(qi, -1, -1):
            lsn_j = lsn[:, j * TILE:(j + 1) * TILE]
            if j == qi:
                lsn_j = jnp.where(strict, lsn_j, 0.0)
            hi, lo = _split_bf16(lsn_j, 2)
            cum = _dot(jnp.concatenate([hi, lo], axis=1), from_key2)
            a = jnp.exp2(z2[:, j * TILE:(j + 1) * TILE] + cum + tail)
            if j == qi:
                a = jnp.where(strict, a, 0.0)
            weights[j] = a.astype(BF16)
            tail = tail + cum[:, 0:1]
        acc = _dot(jnp.concatenate(weights, axis=1), v_ref[0, :n, :])
        y_ref[0, rows, :] = _gated(_unstack_heads(acc), g_ref, rows)


def _rotary(t, cos, sin_signed):
    lane = lax.broadcasted_iota(jnp.int32, t.shape, 1)
    half = HEAD_DIM // 2
    partner = jnp.where((lane & (HEAD_DIM - 1)) < half,
                        pltpu.roll(t, LANES - half, 1),
                        pltpu.roll(t, half, 1))
    return t * cos + partner * sin_signed


def _rotary_t(t, cos_t, sin_signed_t):
    half = HEAD_DIM // 2
    blocks = [t[i * half:(i + 1) * half] for i in range(LANES // half)]
    partner = jnp.concatenate([blocks[i ^ 1] for i in range(len(blocks))], axis=0)
    return t * cos_t + partner * sin_signed_t


def _ret_body(pair, lg_ref, q_ref, kt_ref, v_ref, g_ref, cos_ref, sin_ref, cost_ref, sint_ref,
              gain_ref, y_ref):
    s_len = q_ref.shape[1]
    nt = s_len // TILE
    lg0 = lg_ref[PAIR * pair]
    lg1 = lg_ref[PAIR * pair + 1]

    lane_row = lax.broadcasted_iota(jnp.int32, (1, LANES), 1)
    lg_lane = jnp.where(lane_row < HEAD_DIM, lg0, lg1)
    idx = lax.broadcasted_iota(jnp.int32, (TILE, LANES), 0).astype(F32)
    query_decay = jnp.exp((idx + 1.0) * lg_lane)
    chunk_decay = jnp.exp(float(TILE) * lg_lane)
    sub_row = lax.broadcasted_iota(jnp.int32, (LANES, TILE), 0)
    pos_t = lax.broadcasted_iota(jnp.int32, (LANES, TILE), 1).astype(F32)
    key_decay_t = jnp.exp((TILE - 1.0 - pos_t) * jnp.where(sub_row < HEAD_DIM, lg0, lg1))

    r = lax.broadcasted_iota(jnp.int32, (TILE, TILE), 0)
    c = lax.broadcasted_iota(jnp.int32, (TILE, TILE), 1)
    diff = jnp.maximum(r - c, 0).astype(F32)
    intra_decay = jnp.concatenate(
        [jnp.where(r >= c, jnp.exp(diff * lg), 0.0) for lg in (lg0, lg1)], axis=0)

    rr = lax.broadcasted_iota(jnp.int32, (LANES, LANES), 0)
    cc = lax.broadcasted_iota(jnp.int32, (LANES, LANES), 1)
    same_head = (rr < HEAD_DIM) == (cc < HEAD_DIM)
    lane = lax.broadcasted_iota(jnp.int32, (TILE, LANES), 1)
    first_head = lane < HEAD_DIM

    def head_mean(t):
        s0 = jnp.sum(jnp.where(first_head, t, 0.0), axis=-1, keepdims=True)
        s_all = jnp.sum(t, axis=-1, keepdims=True)
        return jnp.where(first_head, s0, s_all - s0) * (1.0 / HEAD_DIM)

    state = jnp.zeros((LANES, LANES), F32)
    for i in range(nt):
        rows = slice(i * TILE, (i + 1) * TILE)
        qc = _rotary(q_ref[0, rows, :].astype(F32), cos_ref[rows, :], sin_ref[rows, :]).astype(BF16)
        kt = _rotary_t(kt_ref[:, rows].astype(F32), cost_ref[:, rows], sint_ref[:, rows])
        vc = v_ref[0, rows, :]
        scores = _dot(_stack_heads(qc), kt.astype(BF16)) * intra_decay
        o = _unstack_heads(_dot(scores.astype(BF16), vc))
        o = o + _dot(qc, state.astype(BF16)) * query_decay
        kd = (kt * key_decay_t).astype(BF16)
        state = chunk_decay * state + jnp.where(same_head, _dot(kd, vc), 0.0)

        d = o - head_mean(o)
        var = head_mean(d * d)
        y_ref[0, rows, :] = _gated(d * lax.rsqrt(var + GN_EPS) * gain_ref[...], g_ref, rows)


def _mixer_kernel(lg_ref, q_ref, kt_ref, v_ref, g_ref, ft_ref, bf_ref, cos_ref, sin_ref, cost_ref,
                  sint_ref, gain_ref, y_ref, c_scr, vaug_scr):
    pair = pl.program_id(1)
    ret0 = N_FOX_PAIRS
    sb0 = N_FOX_PAIRS + N_RET_PAIRS

    @pl.when(pair < ret0)
    def _():
        _fox_body(pair, q_ref, kt_ref, v_ref, g_ref, ft_ref, bf_ref, y_ref, c_scr, vaug_scr)

    @pl.when(jnp.logical_and(pair >= ret0, pair < sb0))
    def _():
        _ret_body(pair - ret0, lg_ref, q_ref, kt_ref, v_ref, g_ref, cos_ref, sin_ref, cost_ref,
                  sint_ref, gain_ref, y_ref)

    @pl.when(pair >= sb0)
    def _():
        _sb_body(q_ref, kt_ref, v_ref, g_ref, y_ref)


def _mixer(qvg, kt, f_t, b_f, log_g, tables, gain, batch, s_len):
    blk = (1, s_len, LANES)
    tab = pl.BlockSpec((s_len, LANES), lambda b, p: (0, 0))
    tab_t = pl.BlockSpec((LANES, s_len), lambda b, p: (0, 0))
    return pl.pallas_call(
        _mixer_kernel,
        grid=(batch, N_PAIRS),
        in_specs=[
            pl.BlockSpec(memory_space=pltpu.SMEM),
            pl.BlockSpec(blk, lambda b, p: (b, 0, p)),
            pl.BlockSpec((LANES, s_len), lambda b, p: (p, b)),
            pl.BlockSpec(blk, lambda b, p: (b, 0, N_PAIRS + p)),
            pl.BlockSpec(blk, lambda b, p: (b, 0, 2 * N_PAIRS + p)),
            pl.BlockSpec((F_ROWS, s_len), lambda b, p: (0, b)),
            pl.BlockSpec((F_ROWS, 1), lambda b, p: (0, 0)),
            tab, tab, tab_t, tab_t,
            pl.BlockSpec((1, LANES),
                         lambda b, p: (0, jnp.clip(p - N_FOX_PAIRS, 0, N_RET_PAIRS - 1))),
        ],
        out_specs=pl.BlockSpec(blk, lambda b, p: (b, 0, p)),
        out_shape=jax.ShapeDtypeStruct((batch, s_len, D_MIX), BF16),
        scratch_shapes=[pltpu.VMEM((F_ROWS, s_len), F32),
                        pltpu.VMEM((s_len, 2 * LANES), BF16)],
        compiler_params=pltpu.CompilerParams(
            dimension_semantics=("parallel", "parallel"), vmem_limit_bytes=VMEM_LIMIT),
        name="mixer",
    )(log_g, qvg, kt, qvg, qvg, f_t, b_f, *tables, gain)


def _out_kernel(y_ref, x_ref, w_ref, gain_ref, bias_ref, *rest, alpha):
    o_ref = rest[0] if len(rest) == 1 else rest[3]
    h = alpha * x_ref[...] + _dot(y_ref[...], w_ref[...])
    mu = jnp.mean(h, axis=-1, keepdims=True)
    d = h - mu
    var = jnp.mean(d * d, axis=-1, keepdims=True)
    x_next = d * lax.rsqrt(var + LN_EPS) * gain_ref[...] + bias_ref[...]
    o_ref[...] = x_next
    if len(rest) > 1:
        _project(x_next.astype(BF16), *rest[:3], *rest[4:])


def _out(y2d, x2d, w_out, gain, bias, alpha, next_proj=None):
    m, d = x2d.shape
    tm = OUT_TM if next_proj is None else PROJ_TM
    assert m % tm == 0, (m, tm)
    row = lambda n: pl.BlockSpec((tm, n), lambda i: (i, 0))
    in_specs = [row(D_MIX), row(d), _resident((D_MIX, d)), _resident((1, d)), _resident((1, d))]
    out_specs = [row(d)]
    out_shape = [jax.ShapeDtypeStruct((m, d), F32)]
    operands = [y2d, x2d, w_out, gain, bias]
    if next_proj is not None:
        p_in, p_out, p_shape = _proj_specs(m, d, next_proj[0].shape[0])
        in_specs += p_in
        out_specs += p_out
        out_shape += p_shape
        operands += list(next_proj)
    return pl.pallas_call(
        functools.partial(_out_kernel, alpha=alpha),
        grid=(m // tm,),
        in_specs=in_specs,
        out_specs=out_specs,
        out_shape=out_shape,
        compiler_params=pltpu.CompilerParams(
            dimension_semantics=("parallel",), vmem_limit_bytes=VMEM_LIMIT),
        name="out_proj_ln" if next_proj is None else "out_proj_ln_proj",
    )(*operands)


def _rope_tables(s_len):
    half = HEAD_DIM // 2
    inv_freq = (np.float32(1.0)
                / np.power(np.float32(ROPE_BASE), np.arange(half, dtype=np.float32) / np.float32(half)))
    ang = np.arange(s_len, dtype=np.float32)[:, None] * inv_freq[None, :].astype(np.float32)
    cos, sin = np.cos(ang).astype(np.float32), np.sin(ang).astype(np.float32)
    cos_p = np.tile(np.concatenate([cos, cos], axis=-1), (1, PAIR))
    sin_p = np.tile(np.concatenate([-sin, sin], axis=-1), (1, PAIR))
    return tuple(jnp.asarray(t) for t in (cos_p, sin_p, np.ascontiguousarray(cos_p.T),
                                          np.ascontiguousarray(sin_p.T)))


def kernel(x, w_in, b_fgate, ret_gn_gain, w_out, ln_gain, ln_bias):
    batch, s_len, d_model = x.shape
    depth = w_in.shape[0]
    alpha = (2 * depth) ** 0.25
    m = batch * s_len

    tables = _rope_tables(s_len)
    log_g = jnp.asarray(np.log(np.float32(1.0) - np.exp2(
        np.float32(-5.0) - np.arange(N_RET_HEADS, dtype=np.float32))).astype(np.float32))
    head_scale = ([QK_SCALE * LOG2E] * N_FOX_HEADS + [1.0] * N_RET_HEADS
                  + [QK_SCALE * LOG2E] * N_SB_HEADS)
    qscale = jnp.asarray(np.repeat(np.asarray(head_scale, np.float32), HEAD_DIM).reshape(1, D_MIX))

    w_t = jnp.swapaxes(w_in, 1, 2).astype(BF16)
    w_out_b = w_out.astype(BF16)

    def proj_weights(layer):
        wl = w_t[layer]
        wqvg_t = jnp.concatenate([wl[:D_MIX], wl[2 * D_MIX:4 * D_MIX]], axis=0)
        wkf_t = jnp.concatenate(
            [wl[D_MIX:2 * D_MIX], wl[4 * D_MIX:],
             jnp.zeros((F_ROWS - N_FOX_HEADS, d_model), BF16)], axis=0)
        return wqvg_t, wkf_t, qscale

    x2d = x.reshape(m, d_model)
    qvg2d, kt, f_t = _proj(x2d, *proj_weights(0))
    for layer in range(depth):
        b_f = jnp.zeros((F_ROWS, 1), F32).at[:N_FOX_HEADS, 0].set(b_fgate[layer])
        qvg = qvg2d.reshape(batch, s_len, 3 * D_MIX)
        y = _mixer(qvg, kt, f_t, b_f, log_g, tables, ret_gn_gain[layer].reshape(1, -1),
                   batch, s_len)
        next_proj = proj_weights(layer + 1) if layer + 1 < depth else None
        outs = _out(y.reshape(m, D_MIX), x2d, w_out_b[layer],
                    ln_gain[layer].reshape(1, -1), ln_bias[layer].reshape(1, -1), alpha, next_proj)
        x2d = outs[0]
        if next_proj is not None:
            qvg2d, kt, f_t = outs[1:]
    return x2d.reshape(batch, s_len, d_model)
```
